```python
import math
import jax
import jax.numpy as jnp
from jax import lax
import numpy as np

D_MODEL = 2048
BATCH = 4
SEQ = 2048
DEPTH = 4
DEC_BATCH = 128
DEC_SEQ = 4
PAST_LEN = 8192
PAGE_SIZE = 128

F32 = jnp.float32
D_MIX = D_MODEL
GROUP_W = D_MIX // 4
HG_HEADS = 4
HG_DK = GROUP_W // HG_HEADS
HG_DV = GROUP_W // HG_HEADS
HG_CHUNK = 16
DSA_HEADS = 4
DSA_DH = GROUP_W // DSA_HEADS
IDX_HEADS = 8
IDX_DIM = 64
DSA_TOPK = 256
IDX_SCALE = (IDX_DIM * IDX_HEADS) ** -0.5
SSM_HEAD_DIM = 64
SSM_HEADS = GROUP_W // SSM_HEAD_DIM
SSM_GROUPS = 2
SSM_HPG = SSM_HEADS // SSM_GROUPS
SSM_STATE = 128
SSM_CONV = 4
SSM_CHUNK = 64
SSM_XBC = GROUP_W + 2 * SSM_GROUPS * SSM_STATE
MLA_HEADS = 4
MLA_V = GROUP_W // MLA_HEADS
MLA_NOPE = 96
MLA_ROPE = 32
MLA_QLORA = 384
MLA_KVLORA = 128
MLA_SCALE = (MLA_NOPE + MLA_ROPE) ** -0.5
N_MEM = 256
MEM_HEADS = 4
MEM_DH = 128
D_FF = (D_MODEL * 11) // 4
FFN_CONV = 3
ROPE_THETA = 500000.0
Q_BLOCK = 128
EPS = 1e-6
IN_WIDTHS = (GROUP_W, GROUP_W, GROUP_W, GROUP_W,
             DSA_HEADS * DSA_DH, DSA_DH, DSA_DH,
             IDX_HEADS * IDX_DIM, IDX_DIM, IDX_HEADS,
             GROUP_W, SSM_XBC, SSM_HEADS,
             MLA_QLORA, MLA_KVLORA, MLA_ROPE)
D_IN = sum(IN_WIDTHS)

kernel_name = 'hybrid_hgrn2_dsa_ssd_mla_step'


def rms_norm(x, g):
    xf = x.astype(F32)
    y = xf * lax.rsqrt(jnp.mean(xf * xf, axis=-1, keepdims=True) + EPS)
    return (y * g.astype(F32)).astype(x.dtype)


def split_columns(u):
    parts, o = [], 0
    for w in IN_WIDTHS:
        parts.append(u[..., o:o + w])
        o += w
    return parts


def rope_cos_sin(pos, rot_dim, dtype):
    inv = ROPE_THETA ** (-jnp.arange(0, rot_dim, 2, dtype=F32) / rot_dim)
    ang = pos.astype(F32)[:, None] * inv[None, :]
    return jnp.cos(ang).astype(dtype), jnp.sin(ang).astype(dtype)


def rotate(x, cos, sin):
    r2 = cos.shape[-1]
    x1, x2 = x[..., :r2], x[..., r2:2 * r2]
    c, s = cos[:, None, :], sin[:, None, :]
    return jnp.concatenate([x1 * c - x2 * s, x1 * s + x2 * c], axis=-1)


def partial_rotary(x, cos, sin):
    r = 2 * cos.shape[-1]
    return jnp.concatenate([rotate(x[..., :r], cos, sin), x[..., r:]], axis=-1)


def causal_dwconv(x, buf, w, b):
    W, T = w.shape[0], x.shape[1]
    xp = jnp.concatenate([buf.astype(x.dtype), x], axis=1)
    y = b.astype(x.dtype)
    for j in range(W):
        y = y + xp[:, j:j + T] * w[j]
    return y, xp[:, T:]


def pad_time(a, pad):
    return jnp.pad(a, ((0, 0), (0, pad)) + ((0, 0),) * (a.ndim - 2))


def to_chunks(a, n, c):
    return jnp.moveaxis(a.reshape(a.shape[0], n, c, *a.shape[2:]), 1, 0)


def gla_chunked(q, k, v, logf, S0):
    B, T, H, _ = q.shape
    C = HG_CHUNK
    pad = (-T) % C
    n = (T + pad) // C
    q, k, v, logf = (to_chunks(pad_time(a, pad), n, C) for a in (q, k, v, logf))
    tri = jnp.tril(jnp.ones((C, C), dtype=bool))[None, :, :, None, None]

    def step(S, inp):
        qc, kc, vc, gc = inp
        Lc = jnp.cumsum(gc, axis=1)
        decay = jnp.exp(jnp.where(tri, Lc[:, :, None] - Lc[:, None, :], -jnp.inf))
        A = jnp.einsum('bthd,bshd,btshd->bhts', qc, kc, decay)
        o = jnp.einsum('bhts,bshv->bthv', A, vc) + jnp.einsum('bthd,bhdv->bthv', qc * jnp.exp(Lc), S)
        Ll = Lc[:, -1]
        S = jnp.exp(Ll)[..., None] * S + jnp.einsum('bshd,bshv->bhdv', kc * jnp.exp(Ll[:, None] - Lc), vc)
        return S, o

    S, o = lax.scan(step, S0, (q, k, v, logf))
    o = jnp.moveaxis(o, 0, 1).reshape(B, n * C, H, -1)[:, :T]
    return o, S


def hgrn2_mixer(q_raw, f_raw, i_raw, g_raw, lb, gnorm, S0):
    B, T, _ = q_raw.shape
    sh = (B, T, HG_HEADS, HG_DK)
    q = jax.nn.silu(q_raw.astype(F32)).reshape(sh)
    z = f_raw.astype(F32).reshape(sh)
    lbh = lb.astype(F32).reshape(HG_HEADS, HG_DK)
    logf = jnp.logaddexp(jnp.log(lbh), jnp.log1p(-lbh) + jax.nn.log_sigmoid(z))
    k = (1.0 - lbh) * jax.nn.sigmoid(-z)
    v = i_raw.astype(F32).reshape(B, T, HG_HEADS, HG_DV)
    o, S = gla_chunked(q, k, v, logf, S0.astype(F32))
    o = rms_norm(o, gnorm.reshape(HG_HEADS, HG_DV)).reshape(B, T, GROUP_W)
    y = o * jax.nn.silu(g_raw.astype(F32))
    return y.astype(q_raw.dtype), S


def ssd_chunked(x, dt, A, Bm, Cm, h0):
    B, T = x.shape[:2]
    C = SSM_CHUNK
    pad = (-T) % C
    n = (T + pad) // C
    x, dt, Bm, Cm = (to_chunks(pad_time(a, pad), n, C) for a in (x, dt, Bm, Cm))
    tri = jnp.tril(jnp.ones((C, C), dtype=bool))[None, :, :, None, None]

    def step(h, inp):
        xc, dtc, Bc, Cc = inp
        Lc = jnp.cumsum(dtc * A, axis=1)
        Lmat = jnp.exp(jnp.where(tri, Lc[:, :, None] - Lc[:, None, :], -jnp.inf))
        CB = jnp.einsum('btgn,bsgn->btsg', Cc, Bc)
        M = CB[..., None] * Lmat * dtc[:, None]
        y = jnp.einsum('btsgh,bsghp->btghp', M, xc) + jnp.einsum('btgn,bghpn->btghp', Cc, h) * jnp.exp(Lc)[..., None]
        Ll = Lc[:, -1]
        w = jnp.exp(Ll[:, None] - Lc) * dtc
        h = jnp.exp(Ll)[..., None, None] * h + jnp.einsum('bsgh,bsghp,bsgn->bghpn', w, xc, Bc)
        return h, y

    h, y = lax.scan(step, h0, (x, dt, Bm, Cm))
    y = jnp.moveaxis(y, 0, 1).reshape(B, n * C, SSM_GROUPS, SSM_HPG, SSM_HEAD_DIM)[:, :T]
    return y, h


def ssd_mixer(z, xbc_raw, dt_raw, conv_buf, conv_w, conv_b, a_log, dt_bias, d_skip, gnorm, h0):
    B, T, _ = z.shape
    xbc, new_buf = causal_dwconv(xbc_raw, conv_buf, conv_w, conv_b)
    xbc = jax.nn.silu(xbc.astype(F32))
    nb = SSM_GROUPS * SSM_STATE
    xs = xbc[..., :GROUP_W].reshape(B, T, SSM_GROUPS, SSM_HPG, SSM_HEAD_DIM)
    Bm = xbc[..., GROUP_W:GROUP_W + nb].reshape(B, T, SSM_GROUPS, SSM_STATE)
    Cm = xbc[..., GROUP_W + nb:].reshape(B, T, SSM_GROUPS, SSM_STATE)
    dt = jax.nn.softplus(dt_raw.astype(F32) + dt_bias.astype(F32)).reshape(B, T, SSM_GROUPS, SSM_HPG)
    A = -jnp.exp(a_log.astype(F32)).reshape(SSM_GROUPS, SSM_HPG)
    h0g = h0.astype(F32).reshape(B, SSM_GROUPS, SSM_HPG, SSM_HEAD_DIM, SSM_STATE)
    y, h = ssd_chunked(xs, dt, A, Bm, Cm, h0g)
    y = y + d_skip.astype(F32).reshape(SSM_GROUPS, SSM_HPG)[..., None] * xs
    y = y.reshape(B, T, GROUP_W) * jax.nn.silu(z.astype(F32))
    y = rms_norm(y.reshape(B, T, SSM_GROUPS, GROUP_W // SSM_GROUPS),
                 gnorm.reshape(SSM_GROUPS, GROUP_W // SSM_GROUPS)).reshape(B, T, GROUP_W)
    return y.astype(z.dtype), h.reshape(B, SSM_HEADS, SSM_HEAD_DIM, SSM_STATE), new_buf


def take_rows(a, idx):
    return jax.vmap(lambda r, i: r[i])(a, idx)


def over_query_blocks(fn, arrays, pos):
    B, T = arrays[0].shape[:2]
    nb = T // Q_BLOCK
    blk = tuple(jnp.swapaxes(a.reshape(B, nb, Q_BLOCK, *a.shape[2:]), 0, 1) for a in arrays)
    out = lax.map(lambda args: fn(*args), blk + (pos.reshape(nb, Q_BLOCK),))
    return jnp.swapaxes(out, 0, 1).reshape(B, T, *out.shape[3:])


def dsa_select(iq, iw, ik, qpos, n_keep):
    L = ik.shape[1]
    s = jax.nn.relu(jnp.einsum('bthd,bld->bthl', iq, ik).astype(F32))
    score = jnp.einsum('bth,bthl->btl', iw.astype(F32), s) * IDX_SCALE
    kpos = jnp.arange(L)
    score = jnp.where(kpos[None, None, :] <= qpos[None, :, None], score, -jnp.inf)
    _, idx = lax.top_k(score, n_keep)
    return idx, idx <= qpos[None, :, None]


def sparse_attention(q, kg, vg, valid):
    s = jnp.einsum('bthd,btkd->bhtk', q, kg).astype(F32) * DSA_DH ** -0.5
    s = jnp.where(valid[:, None], s, -jnp.inf)
    p = jax.nn.softmax(s, axis=-1).astype(vg.dtype)
    return jnp.einsum('bhtk,btkd->bthd', p, vg)


def dsa_prompt(q, k, v, iq, ik, iw, pos):
    n_keep = min(DSA_TOPK, q.shape[1] // 4)

    def block(qb, iqb, iwb, qpos):
        idx, valid = dsa_select(iqb, iwb, ik, qpos, n_keep)
        return sparse_attention(qb, take_rows(k, idx), take_rows(v, idx), valid)

    return over_query_blocks(block, (q, iq, iw), pos)


def mla_attention(q_lat, q_rope, c, kr, qpos, kpos):
    s = (jnp.einsum('bthc,blc->bhtl', q_lat, c) + jnp.einsum('bthr,blr->bhtl', q_rope, kr)).astype(F32) * MLA_SCALE
    s = jnp.where(kpos[None, None, None, :] <= qpos[None, None, :, None], s, -jnp.inf)
    p = jax.nn.softmax(s, axis=-1).astype(c.dtype)
    return jnp.einsum('bhtl,blc->bthc', p, c)


def mla_prompt(q_lat, q_rope, c, kr, pos):
    def block(qb, rb, qpos):
        return mla_attention(qb, rb, c, kr, qpos, pos)
    return over_query_blocks(block, (q_lat, q_rope), pos)


def mem_attention(q, mk, mv):
    s = jnp.einsum('bthd,bnhd->bhtn', q, mk).astype(F32) * MEM_DH ** -0.5
    p = jax.nn.softmax(s, axis=-1).astype(mv.dtype)
    return jnp.einsum('bhtn,bnhd->bthd', p, mv)


def stack_layers(rows, i):
    return jnp.stack([r[i] for r in rows], axis=0)


def setup_inputs(seed: int = 0) -> dict:
    key = jax.random.key(seed)
    keys = jax.random.split(key, 64)
    cnt = [0]

    def nxt():
        k = keys[cnt[0]]
        cnt[0] += 1
        return k

    def nrm(shape, scale=1.0):
        return jax.random.normal(nxt(), shape, F32) * scale

    def gain(shape):
        return 1.0 + nrm(shape, 0.02)

    n_pages = PAST_LEN // PAGE_SIZE
    used = DEC_BATCH * n_pages
    n_pool = used + max(1, used // 4)
    page_table = jax.random.permutation(nxt(), n_pool)[:used].reshape(DEC_BATCH, n_pages).astype(jnp.int32)

    x_prompt = nrm((BATCH, SEQ, D_MODEL))
    x_sample = nrm((DEC_BATCH, DEC_SEQ, D_MODEL))
    mem_prompt = nrm((BATCH, N_MEM, D_MODEL))
    cache_dsa_k = nrm((DEPTH, n_pool, PAGE_SIZE, DSA_DH))
    cache_dsa_v = nrm((DEPTH, n_pool, PAGE_SIZE, DSA_DH))
    cache_dsa_ik = nrm((DEPTH, n_pool, PAGE_SIZE, IDX_DIM))
    cache_mla_c = nrm((DEPTH, n_pool, PAGE_SIZE, MLA_KVLORA))
    cache_mla_kr = nrm((DEPTH, n_pool, PAGE_SIZE, MLA_ROPE))
    cache_mem_k = nrm((DEPTH, DEC_BATCH, N_MEM, MEM_HEADS, MEM_DH))
    cache_mem_v = nrm((DEPTH, DEC_BATCH, N_MEM, MEM_HEADS, MEM_DH))
    state_hgrn = nrm((DEPTH, DEC_BATCH, HG_HEADS, HG_DK, HG_DV), 0.1)
    state_ssm = nrm((DEPTH, DEC_BATCH, SSM_HEADS, SSM_HEAD_DIM, SSM_STATE), 0.1)
    state_ssm_conv = nrm((DEPTH, DEC_BATCH, SSM_CONV - 1, SSM_XBC))
    state_ffn_conv = nrm((DEPTH, DEC_BATCH, FFN_CONV - 1, D_FF))

    dt0 = jnp.exp(jax.random.uniform(nxt(), (DEPTH, SSM_HEADS), F32, math.log(1e-3), math.log(1e-1)))
    ssm_dt_bias = dt0 + jnp.log(-jnp.expm1(-dt0))
    ssm_a_log = jnp.log(jax.random.uniform(nxt(), (DEPTH, SSM_HEADS), F32, 1.0, 16.0))

    return {
        'x_prompt': x_prompt,
        'x_sample': x_sample,
        'cache_dsa_k': cache_dsa_k,
        'cache_dsa_v': cache_dsa_v,
        'cache_dsa_ik': cache_dsa_ik,
        'cache_mla_c': cache_mla_c,
        'cache_mla_kr': cache_mla_kr,
        'cache_mem_k': cache_mem_k,
        'cache_mem_v': cache_mem_v,
        'state_hgrn': state_hgrn,
        'state_ssm': state_ssm,
        'state_ssm_conv': state_ssm_conv,
        'state_ffn_conv': state_ffn_conv,
        'page_table': page_table,
        'mem_prompt': mem_prompt,
        'w_in': nrm((DEPTH, D_MODEL, D_IN), D_MODEL ** -0.5),
        'w_out': nrm((DEPTH, D_MIX, D_MODEL), D_MIX ** -0.5),
        'mix_norm': gain((DEPTH, D_MODEL)),
        'hgrn_lb': nrm((DEPTH, GROUP_W)),
        'hgrn_norm': gain((DEPTH, GROUP_W)),
        'mla_q_norm': gain((DEPTH, MLA_QLORA)),
        'mla_kv_norm': gain((DEPTH, MLA_KVLORA)),
        'mla_w_uq': nrm((DEPTH, MLA_QLORA, MLA_HEADS * (MLA_NOPE + MLA_ROPE)), MLA_QLORA ** -0.5),
        'mla_w_uk': nrm((DEPTH, MLA_KVLORA, MLA_HEADS, MLA_NOPE), MLA_KVLORA ** -0.5),
        'mla_w_uv': nrm((DEPTH, MLA_KVLORA, MLA_HEADS, MLA_V), MLA_KVLORA ** -0.5),
        'ssm_conv_w': nrm((DEPTH, SSM_CONV, SSM_XBC), SSM_CONV ** -0.5),
        'ssm_conv_b': nrm((DEPTH, SSM_XBC), 0.02),
        'ssm_a_log': ssm_a_log,
        'ssm_dt_bias': ssm_dt_bias,
        'ssm_d': 1.0 + nrm((DEPTH, SSM_HEADS), 0.1),
        'ssm_norm': gain((DEPTH, GROUP_W)),
        'mem_norm': gain((DEPTH, D_MODEL)),
        'cross_norm': gain((DEPTH, D_MODEL)),
        'cross_wq': nrm((DEPTH, D_MODEL, MEM_HEADS * MEM_DH), D_MODEL ** -0.5),
        'cross_wk': nrm((DEPTH, D_MODEL, MEM_HEADS * MEM_DH), D_MODEL ** -0.5),
        'cross_wv': nrm((DEPTH, D_MODEL, MEM_HEADS * MEM_DH), D_MODEL ** -0.5),
        'cross_wo': nrm((DEPTH, MEM_HEADS * MEM_DH, D_MODEL), (MEM_HEADS * MEM_DH) ** -0.5),
        'ffn_norm': gain((DEPTH, D_MODEL)),
        'ffn_w_up': nrm((DEPTH, D_MODEL, 2 * D_FF), D_MODEL ** -0.5),
        'ffn_conv_w': nrm((DEPTH, FFN_CONV, D_FF), FFN_CONV ** -0.5),
        'ffn_conv_b': nrm((DEPTH, D_FF), 0.02),
        'ffn_w_down': nrm((DEPTH, D_FF, D_MODEL), D_FF ** -0.5),
        'final_norm': gain((D_MODEL,)),
    }


def reference(x_prompt, x_sample, cache_dsa_k, cache_dsa_v, cache_dsa_ik, cache_mla_c, cache_mla_kr,
              cache_mem_k, cache_mem_v, state_hgrn, state_ssm, state_ssm_conv, state_ffn_conv,
              page_table, mem_prompt,
              w_in, w_out, mix_norm, hgrn_lb, hgrn_norm,
              mla_q_norm, mla_kv_norm, mla_w_uq, mla_w_uk, mla_w_uv,
              ssm_conv_w, ssm_conv_b, ssm_a_log, ssm_dt_bias, ssm_d, ssm_norm,
              mem_norm, cross_norm, cross_wq, cross_wk, cross_wv, cross_wo,
              ffn_norm, ffn_w_up, ffn_conv_w, ffn_conv_b, ffn_w_down, final_norm):
    lb = jnp.cumsum(jax.nn.softmax(hgrn_lb.astype(F32), axis=0), axis=0)
    lb = lb - lb[:1]

    def layer(x, l, pos, hg0, ssm0, sconv0, fconv0, mk, mv, dsa_attend, mla_attend):
        B, T, _ = x.shape
        hn = rms_norm(x, mix_norm[l])
        (hq, hf, hi, hgt, dq, dk, dv, iq, ik, iw, sz, sxbc, sdt, mcq, mckv, mkr) = split_columns(hn @ w_in[l])
        cos_d, sin_d = rope_cos_sin(pos, DSA_DH // 4, x.dtype)
        cos_i, sin_i = rope_cos_sin(pos, IDX_DIM // 4, x.dtype)
        cos_m, sin_m = rope_cos_sin(pos, MLA_ROPE, x.dtype)
        y_a, hg1 = hgrn2_mixer(hq, hf, hi, hgt, lb[l], hgrn_norm[l], hg0)
        q_b = partial_rotary(dq.reshape(B, T, DSA_HEADS, DSA_DH), cos_d, sin_d)
        k_b = partial_rotary(dk.reshape(B, T, 1, DSA_DH), cos_d, sin_d)[:, :, 0]
        iq_b = partial_rotary(iq.reshape(B, T, IDX_HEADS, IDX_DIM), cos_i, sin_i)
        ik_b = partial_rotary(ik.reshape(B, T, 1, IDX_DIM), cos_i, sin_i)[:, :, 0]
        y_b = dsa_attend(q_b, k_b, dv, iq_b, ik_b, iw, pos).reshape(B, T, GROUP_W)
        y_c, ssm1, sconv1 = ssd_mixer(sz, sxbc, sdt, sconv0, ssm_conv_w[l], ssm_conv_b[l], ssm_a_log[l],
                                      ssm_dt_bias[l], ssm_d[l], ssm_norm[l], ssm0)
        cq = rms_norm(mcq, mla_q_norm[l])
        q_d = (cq @ mla_w_uq[l]).reshape(B, T, MLA_HEADS, MLA_NOPE + MLA_ROPE)
        q_lat = jnp.einsum('bthn,chn->bthc', q_d[..., :MLA_NOPE], mla_w_uk[l])
        q_rope = rotate(q_d[..., MLA_NOPE:], cos_m, sin_m)
        c_kv = rms_norm(mckv, mla_kv_norm[l])
        k_rope = rotate(mkr.reshape(B, T, 1, MLA_ROPE), cos_m, sin_m)[:, :, 0]
        o_lat = mla_attend(q_lat, q_rope, c_kv, k_rope, pos)
        y_d = jnp.einsum('bthc,chv->bthv', o_lat, mla_w_uv[l]).reshape(B, T, GROUP_W)
        x = x + jnp.concatenate([y_a, y_b, y_c, y_d], axis=-1) @ w_out[l]
        qc = (rms_norm(x, cross_norm[l]) @ cross_wq[l]).reshape(B, T, MEM_HEADS, MEM_DH)
        x = x + mem_attention(qc, mk, mv).reshape(B, T, MEM_HEADS * MEM_DH) @ cross_wo[l]
        up = rms_norm(x, ffn_norm[l]) @ ffn_w_up[l]
        a, fconv1 = causal_dwconv(up[..., :D_FF], fconv0, ffn_conv_w[l], ffn_conv_b[l])
        x = x + (jax.nn.silu(a) * up[..., D_FF:]) @ ffn_w_down[l]
        return x, (k_b, dv, ik_b, c_kv, k_rope, hg1, ssm1, sconv1, fconv1)

    Bp, Tp = x_prompt.shape[:2]
    Bs, Ts = x_sample.shape[:2]
    pos_p = jnp.arange(Tp, dtype=jnp.int32)
    pos_s = PAST_LEN + jnp.arange(Ts, dtype=jnp.int32)
    kpos_s = jnp.arange(PAST_LEN + Ts, dtype=jnp.int32)
    hp, hs = x_prompt, x_sample
    p_rows, s_rows, p_mk, p_mv = [], [], [], []
    for l in range(DEPTH):
        mn = rms_norm(mem_prompt, mem_norm[l])
        mk = (mn @ cross_wk[l]).reshape(Bp, N_MEM, MEM_HEADS, MEM_DH)
        mv = (mn @ cross_wv[l]).reshape(Bp, N_MEM, MEM_HEADS, MEM_DH)
        hp, rows = layer(hp, l, pos_p,
                         jnp.zeros((Bp, HG_HEADS, HG_DK, HG_DV), F32),
                         jnp.zeros((Bp, SSM_HEADS, SSM_HEAD_DIM, SSM_STATE), F32),
                         jnp.zeros((Bp, SSM_CONV - 1, SSM_XBC), hp.dtype),
                         jnp.zeros((Bp, FFN_CONV - 1, D_FF), hp.dtype),
                         mk, mv, dsa_prompt, mla_prompt)
        p_rows.append(rows)
        p_mk.append(mk)
        p_mv.append(mv)

        def dsa_sample(q, k, v, iq, ik, iw, pos):
            n_keep = min(DSA_TOPK, (PAST_LEN + Ts) // 4)
            ik_past = cache_dsa_ik[l, page_table].reshape(Bs, PAST_LEN, IDX_DIM).astype(ik.dtype)
            idx, valid = dsa_select(iq, iw, jnp.concatenate([ik_past, ik], axis=1), pos, n_keep)
            in_past = (idx < PAST_LEN)[..., None]
            pidx = jnp.minimum(idx, PAST_LEN - 1)
            phys = jax.vmap(lambda pt, i: pt[i])(page_table, pidx // PAGE_SIZE)
            off = pidx % PAGE_SIZE
            nidx = jnp.clip(idx - PAST_LEN, 0, Ts - 1)
            kg = jnp.where(in_past, cache_dsa_k[l, phys, off].astype(k.dtype), take_rows(k, nidx))
            vg = jnp.where(in_past, cache_dsa_v[l, phys, off].astype(v.dtype), take_rows(v, nidx))
            return sparse_attention(q, kg, vg, valid)

        def mla_sample(q_lat, q_rope, c, kr, pos):
            c_all = jnp.concatenate(
                [cache_mla_c[l, page_table].reshape(Bs, PAST_LEN, MLA_KVLORA).astype(c.dtype), c], axis=1)
            kr_all = jnp.concatenate(
                [cache_mla_kr[l, page_table].reshape(Bs, PAST_LEN, MLA_ROPE).astype(kr.dtype), kr], axis=1)
            return mla_attention(q_lat, q_rope, c_all, kr_all, pos, kpos_s)

        hs, rows = layer(hs, l, pos_s, state_hgrn[l], state_ssm[l], state_ssm_conv[l], state_ffn_conv[l],
                         cache_mem_k[l], cache_mem_v[l], dsa_sample, mla_sample)
        s_rows.append(rows)

    y_prompt = rms_norm(hp, final_norm)
    y_sample = rms_norm(hs, final_norm)
    p_dsa_k, p_dsa_v, p_dsa_ik = stack_layers(p_rows, 0), stack_layers(p_rows, 1), stack_layers(p_rows, 2)
    p_mla_c, p_mla_kr = stack_layers(p_rows, 3), stack_layers(p_rows, 4)
    p_hgrn, p_ssm = stack_layers(p_rows, 5), stack_layers(p_rows, 6)
    p_ssm_conv, p_ffn_conv = stack_layers(p_rows, 7), stack_layers(p_rows, 8)
    p_mem_k, p_mem_v = jnp.stack(p_mk, axis=0), jnp.stack(p_mv, axis=0)
    s_dsa_k, s_dsa_v, s_dsa_ik = stack_layers(s_rows, 0), stack_layers(s_rows, 1), stack_layers(s_rows, 2)
    s_mla_c, s_mla_kr = stack_layers(s_rows, 3), stack_layers(s_rows, 4)
    s_hgrn, s_ssm = stack_layers(s_rows, 5), stack_layers(s_rows, 6)
    s_ssm_conv, s_ffn_conv = stack_layers(s_rows, 7), stack_layers(s_rows, 8)
    return (y_prompt, y_sample,
            p_dsa_k, p_dsa_v, p_dsa_ik, p_mla_c, p_mla_kr, p_hgrn, p_ssm, p_ssm_conv, p_ffn_conv, p_mem_k, p_mem_v,
            s_dsa_k, s_dsa_v, s_dsa_ik, s_mla_c, s_mla_kr, s_hgrn, s_ssm, s_ssm_conv, s_ffn_conv)
```

```python
import functools
import math

import jax
import jax.numpy as jnp
from jax import lax
from jax.experimental import pallas as pl
from jax.experimental.pallas import tpu as pltpu

F32 = jnp.float32
BF16 = jnp.bfloat16
I32 = jnp.int32

LANE = 128
SUBLANE = 8
VMEM_LIMIT = 56 * 1024 * 1024

D_MODEL = 2048
DEPTH = 4
PAGE_SIZE = 128
GROUP_W = D_MODEL // 4
HG_HEADS = 4
HG_DK = GROUP_W // HG_HEADS
HG_CHUNK = 16
DSA_HEADS = 4
DSA_DH = GROUP_W // DSA_HEADS
IDX_HEADS = 8
IDX_DIM = 64
DSA_TOPK = 256
IDX_SCALE = (IDX_DIM * IDX_HEADS) ** -0.5
SSM_HEAD_DIM = 64
SSM_HEADS = GROUP_W // SSM_HEAD_DIM
SSM_GROUPS = 2
SSM_HPG = SSM_HEADS // SSM_GROUPS
SSM_STATE = 128
SSM_CONV = 4
SSM_XBC = GROUP_W + 2 * SSM_GROUPS * SSM_STATE
SSM_CHUNK = 128
MLA_HEADS = 4
MLA_V = GROUP_W // MLA_HEADS
MLA_NOPE = 96
MLA_ROPE = 32
MLA_QLORA = 384
MLA_KVLORA = 128
MLA_SCALE = (MLA_NOPE + MLA_ROPE) ** -0.5
N_MEM = 256
MEM_HEADS = 4
MEM_DH = 128
D_FF = (D_MODEL * 11) // 4
FFN_CONV = 3
ROPE_THETA = 500000.0
EPS = 1e-6
IN_WIDTHS = (GROUP_W, GROUP_W, GROUP_W, GROUP_W,
             DSA_HEADS * DSA_DH, DSA_DH, DSA_DH,
             IDX_HEADS * IDX_DIM, IDX_DIM, IDX_HEADS,
             GROUP_W, SSM_XBC, SSM_HEADS,
             MLA_QLORA, MLA_KVLORA, MLA_ROPE)
NEG_INF = float("-inf")


def _cparams(*sem):
    return pltpu.CompilerParams(dimension_semantics=sem, vmem_limit_bytes=VMEM_LIMIT)


def _pick_tile(n, cap, unit=LANE):
    if n <= cap:
        return n
    best = None
    for t in range(unit, cap + 1, unit):
        if n % t == 0:
            best = t
    assert best is not None, (n, cap)
    return best


def _silu(x):
    return x * (1.0 / (1.0 + jnp.exp(-x)))


def _softplus(x):
    return jnp.maximum(x, 0.0) + jnp.log(1.0 + jnp.exp(-jnp.abs(x)))


def _log_sigmoid(x):
    return jnp.minimum(x, 0.0) - jnp.log(1.0 + jnp.exp(-jnp.abs(x)))


def _rms(x, g):
    return x * lax.rsqrt(jnp.mean(x * x, axis=-1, keepdims=True) + EPS) * g


def _mm_body(*refs, norm, has_res, stage):
    refs = list(refs)
    x_ref = refs.pop(0)
    g_ref = refs.pop(0) if norm else None
    w_ref = refs.pop(0)
    res_ref = refs.pop(0) if has_res else None
    o_ref = refs.pop(0)
    if stage:
        xb_ref = refs.pop(0)

        @pl.when(pl.program_id(1) == 0)
        def _():
            x = x_ref[...].astype(F32)
            if norm:
                x = _rms(x, g_ref[...])
            xb_ref[...] = x.astype(BF16)

        xb = xb_ref[...]
    else:
        xb = x_ref[...]
    acc = jnp.dot(xb, w_ref[...], preferred_element_type=F32)
    if has_res:
        acc = acc + res_ref[...]
    o_ref[...] = acc.astype(o_ref.dtype)


def _mm(x, w, gain=None, res=None, out_dtype=F32, tm_cap=512, tn_cap=1024, name="mm"):
    M, K = x.shape
    N = w.shape[1]
    tm = _pick_tile(M, tm_cap, SUBLANE)
    tn = _pick_tile(N, tn_cap)
    norm = gain is not None
    stage = norm or x.dtype != BF16
    in_specs = [pl.BlockSpec((tm, K), lambda i, j: (i, 0))]
    args = [x]
    if norm:
        in_specs.append(pl.BlockSpec((1, K), lambda i, j: (0, 0)))
        args.append(gain.reshape(1, K).astype(F32))
    in_specs.append(pl.BlockSpec((K, tn), lambda i, j: (0, j)))
    args.append(w)
    if res is not None:
        in_specs.append(pl.BlockSpec((tm, tn), lambda i, j: (i, j)))
        args.append(res)
    return pl.pallas_call(
        functools.partial(_mm_body, norm=norm, has_res=res is not None, stage=stage),
        grid=(M // tm, N // tn),
        in_specs=in_specs,
        out_specs=pl.BlockSpec((tm, tn), lambda i, j: (i, j)),
        out_shape=jax.ShapeDtypeStruct((M, N), out_dtype),
        scratch_shapes=[pltpu.VMEM((tm, K), BF16)] if stage else [],
        compiler_params=_cparams("parallel", "arbitrary"),
        name=name,
    )(*args)


def _norm_body(x_ref, g_ref, o_ref):
    o_ref[...] = _rms(x_ref[...], g_ref[...])


def _rmsnorm(x, gain, tm_cap=512):
    M, K = x.shape
    tm = _pick_tile(M, tm_cap, SUBLANE)
    return pl.pallas_call(
        _norm_body,
        grid=(M // tm,),
        in_specs=[pl.BlockSpec((tm, K), lambda i: (i, 0)), pl.BlockSpec((1, K), lambda i: (0, 0))],
        out_specs=pl.BlockSpec((tm, K), lambda i: (i, 0)),
        out_shape=jax.ShapeDtypeStruct((M, K), F32),
        compiler_params=_cparams("parallel"),
        name="rmsnorm",
    )(x, gain.reshape(1, K).astype(F32))


def _ffn_up_body(x_ref, g_ref, wa_ref, wb_ref, cw_ref, cb_ref, prev_ref, h_ref, tail_ref,
                 xb_ref, halo_ref, *, seq, tm, tn):
    i = pl.program_id(0)
    j = pl.program_id(1)

    @pl.when(j == 0)
    def _():
        xb_ref[...] = _rms(x_ref[...], g_ref[...]).astype(BF16)

    xb = xb_ref[...]
    a = jnp.dot(xb, wa_ref[...], preferred_element_type=F32)
    b = jnp.dot(xb, wb_ref[...], preferred_element_type=F32)
    row = lax.broadcasted_iota(I32, (tm, 1), 0)
    if seq >= tm:
        col = pl.multiple_of(j * tn, tn)

        @pl.when((i % (seq // tm)) == 0)
        def _():
            halo_ref[:, pl.ds(col, tn)] = prev_ref[0]

        halo = halo_ref[:, pl.ds(col, tn)]
        p1 = jnp.where(row == 0, halo[1:2], pltpu.roll(a, 1, 0))
        p2 = jnp.where(row == 0, halo[0:1], jnp.where(row == 1, halo[1:2], pltpu.roll(a, 2, 0)))
        halo_ref[:, pl.ds(col, tn)] = a[tm - 2:tm]
        tail_ref[0, :, pl.ds(col, tn)] = a[tm - 2:tm]
    else:
        t = row % seq
        prev = prev_ref[...]
        p1 = jnp.where(t == 0, prev[1], pltpu.roll(a, 1, 0))
        p2 = jnp.where(t == 0, prev[0], jnp.where(t == 1, prev[1], pltpu.roll(a, 2, 0)))
        tail_ref[...] = a
    cw = cw_ref[...]
    conv = cb_ref[...] + cw[0:1] * p2 + cw[1:2] * p1 + cw[2:3] * a
    h_ref[...] = (_silu(conv) * b).astype(h_ref.dtype)


def _ffn_up(x, gain, w_up, conv_w, conv_b, prev, seq, tm_cap=512, tn_cap=512):
    M, D = x.shape
    Fh = w_up.shape[1] // 2
    tm = _pick_tile(M, tm_cap, SUBLANE)
    tn = _pick_tile(Fh, tn_cap)
    nj = Fh // tn
    big = seq >= tm
    if big:
        assert seq % tm == 0
        per = seq // tm
        prev_spec = pl.BlockSpec((1, 2, tn), lambda i, j: (i // per, 0, j))
        tail_spec = pl.BlockSpec((1, 2, Fh), lambda i, j: (i // per, 0, 0))
        tail_shape = jax.ShapeDtypeStruct((M // seq, 2, Fh), F32)
    else:
        assert tm % seq == 0
        prev_spec = pl.BlockSpec((2, tm, tn), lambda i, j: (0, i, j))
        tail_spec = pl.BlockSpec((tm, tn), lambda i, j: (i, j))
        tail_shape = jax.ShapeDtypeStruct((M, Fh), F32)
    return pl.pallas_call(
        functools.partial(_ffn_up_body, seq=seq, tm=tm, tn=tn),
        grid=(M // tm, nj),
        in_specs=[
            pl.BlockSpec((tm, D), lambda i, j: (i, 0)),
            pl.BlockSpec((1, D), lambda i, j: (0, 0)),
            pl.BlockSpec((D, tn), lambda i, j: (0, j)),
            pl.BlockSpec((D, tn), lambda i, j: (0, j + nj)),
            pl.BlockSpec((FFN_CONV, tn), lambda i, j: (0, j)),
            pl.BlockSpec((1, tn), lambda i, j: (0, j)),
            prev_spec,
        ],
        out_specs=[pl.BlockSpec((tm, tn), lambda i, j: (i, j)), tail_spec],
        out_shape=[jax.ShapeDtypeStruct((M, Fh), BF16), tail_shape],
        scratch_shapes=[pltpu.VMEM((tm, D), BF16), pltpu.VMEM((2, Fh), F32)],
        compiler_params=_cparams("arbitrary", "arbitrary"),
        name="ffn_up",
    )(x, gain.reshape(1, D).astype(F32), w_up, w_up, conv_w.astype(F32),
      conv_b.reshape(1, Fh).astype(F32), prev)


def _hgrn_body(q_ref, f_ref, i_ref, g_ref, loglb_ref, log1mlb_ref, onemlb_ref, gn_ref, s0_ref,
               y_ref, s1_ref, *, T, C, t_real):
    loglb = loglb_ref[...]
    log1mlb = log1mlb_ref[...]
    onemlb = onemlb_ref[...]
    gn = gn_ref[...]
    row = lax.broadcasted_iota(I32, (C, 1), 0)
    tri = (lax.broadcasted_iota(I32, (C, C, 1), 0) >= lax.broadcasted_iota(I32, (C, C, 1), 1))

    def chunk(c, St):
        r0 = pl.multiple_of(c * C, C)
        q = _silu(q_ref[0, pl.ds(r0, C), :])
        z = f_ref[0, pl.ds(r0, C), :]
        v = i_ref[0, pl.ds(r0, C), :]
        b_ = log1mlb + _log_sigmoid(z)
        logf = jnp.maximum(loglb, b_) + jnp.log(1.0 + jnp.exp(-jnp.abs(loglb - b_)))
        e = jnp.exp(-jnp.abs(z))
        k = onemlb * (jnp.where(z >= 0, e, 1.0) / (1.0 + e))
        if t_real < T:
            live = (r0 + row) < t_real
            logf = jnp.where(live, logf, 0.0)
            k = jnp.where(live, k, 0.0)
        Lc = logf
        sh = 1
        while sh < C:
            Lc = Lc + jnp.where(row >= sh, pltpu.roll(Lc, sh, 0), 0.0)
            sh *= 2
        decay = jnp.exp(jnp.where(tri, Lc[:, None, :] - Lc[None, :, :], NEG_INF))
        A = jnp.sum(q[:, None, :] * k[None, :, :] * decay, axis=-1)
        o = jnp.dot(A.astype(BF16), v.astype(BF16), preferred_element_type=F32)
        o = o + lax.dot_general((q * jnp.exp(Lc)).astype(BF16), St.astype(BF16),
                                (((1,), (1,)), ((), ())), preferred_element_type=F32)
        Ll = Lc[C - 1:C, :]
        kd = k * jnp.exp(Ll - Lc)
        St = St * jnp.exp(Ll) + lax.dot_general(v.astype(BF16), kd.astype(BF16),
                                                (((0,), (0,)), ((), ())), preferred_element_type=F32)
        y = _rms(o, gn) * _silu(g_ref[0, pl.ds(r0, C), :])
        y_ref[0, pl.ds(r0, C), :] = y
        return St

    St = lax.fori_loop(0, T // C, chunk, s0_ref[0, 0].T)
    s1_ref[0, 0] = St.T


def _hgrn(hq, hf, hi, hg, lb, gnorm, S0, C, t_real):
    B, T, _ = hq.shape
    dk = HG_DK
    lb = lb.astype(F32).reshape(1, GROUP_W)
    consts = [jnp.log(lb), jnp.log1p(-lb), 1.0 - lb, gnorm.astype(F32).reshape(1, GROUP_W)]
    tok = pl.BlockSpec((1, T, dk), lambda b, h: (b, 0, h))
    cst = pl.BlockSpec((1, dk), lambda b, h: (0, h))
    st = pl.BlockSpec((1, 1, dk, dk), lambda b, h: (b, h, 0, 0))
    return pl.pallas_call(
        functools.partial(_hgrn_body, T=T, C=C, t_real=t_real),
        grid=(B, HG_HEADS),
        in_specs=[tok, tok, tok, tok, cst, cst, cst, cst, st],
        out_specs=[tok, st],
        out_shape=[jax.ShapeDtypeStruct((B, T, GROUP_W), F32),
                   jax.ShapeDtypeStruct((B, HG_HEADS, dk, dk), F32)],
        compiler_params=_cparams("parallel", "parallel"),
        name="hgrn",
    )(hq, hf, hi, hg, *consts, S0)


def _split3(a):
    a1 = a.astype(BF16)
    r = a - a1.astype(F32)
    a2 = r.astype(BF16)
    a3 = (r - a2.astype(F32)).astype(BF16)
    return a1, a2, a3


def _dot3(m01, a, dims):
    out = None
    for part in _split3(a):
        lhs, rhs = (m01, part) if dims == "mn" else (part, m01)
        dn = (((1,), (0,)), ((), ())) if dims == "mn" else (((0,), (0,)), ((), ()))
        t = lax.dot_general(lhs, rhs, dn, preferred_element_type=F32)
        out = t if out is None else out + t
    return out


def _ssd_body(z_ref, x_ref, dt_ref, buf_ref, cw_ref, cb_ref, A_ref, dtb_ref, dsk_ref, gn_ref, h0_ref,
              y_ref, h1_ref, nbuf_ref, xp_ref, h_ref, *, T, C, t_real):
    P, N, G, HPG = SSM_HEAD_DIM, SSM_STATE, SSM_GROUPS, SSM_HPG
    H = SSM_HEADS
    PAD = SUBLANE
    xp_ref[0:PAD, :] = jnp.zeros((PAD, SSM_XBC), F32)
    xp_ref[PAD - (SSM_CONV - 1):PAD, :] = buf_ref[0]
    xp_ref[PAD:PAD + T, :] = x_ref[0]
    nbuf_ref[0] = xp_ref[PAD + t_real - (SSM_CONV - 1):PAD + t_real, :]
    h_ref[...] = h0_ref[0]
    cw = cw_ref[...]
    cb = cb_ref[...]
    Ah = A_ref[...]
    dtb = dtb_ref[...]
    dsk = dsk_ref[...]
    gn = gn_ref[...]
    r_i = lax.broadcasted_iota(I32, (C, C), 0)
    c_i = lax.broadcasted_iota(I32, (C, C), 1)
    tril = r_i >= c_i
    tril_b = tril.astype(BF16)
    triu_b = (r_i <= c_i).astype(BF16)
    eye_b = (r_i == c_i).astype(BF16)
    rowv = lax.broadcasted_iota(I32, (C, 1), 0)

    def chunk(c, carry):
        r0 = pl.multiple_of(c * C, C)
        X = xp_ref[pl.ds(r0, C + PAD), :]
        conv = cb + cw[3:4] * X[PAD:PAD + C] + cw[2:3] * X[PAD - 1:PAD - 1 + C] \
            + cw[1:2] * X[PAD - 2:PAD - 2 + C] + cw[0:1] * X[PAD - 3:PAD - 3 + C]
        xbc = _silu(conv)
        dt = _softplus(dt_ref[0, pl.ds(r0, C), :] + dtb)
        if t_real < T:
            dt = jnp.where((r0 + rowv) < t_real, dt, 0.0)
        a = dt * Ah
        Lc_col = _dot3(tril_b, a, "mn")
        Lc_row = _dot3(triu_b, a, "tn")
        dt_row = _dot3(eye_b, dt, "tn")
        zc = z_ref[0, pl.ds(r0, C), :]
        ys = []
        for g in range(G):
            Bm = xbc[:, GROUP_W + g * N:GROUP_W + (g + 1) * N]
            Cm = xbc[:, GROUP_W + G * N + g * N:GROUP_W + G * N + (g + 1) * N]
            Bb = Bm.astype(BF16)
            Cb = Cm.astype(BF16)
            CB = lax.dot_general(Cb, Bb, (((1,), (1,)), ((), ())), preferred_element_type=F32)
            for hh in range(HPG):
                h = g * HPG + hh
                xh = xbc[:, h * P:(h + 1) * P]
                lcol = Lc_col[:, h:h + 1]
                lrow = Lc_row[h:h + 1, :]
                Lmat = jnp.exp(jnp.where(tril, lcol - lrow, NEG_INF))
                Mm = CB * Lmat * dt_row[h:h + 1, :]
                hs = h_ref[h]
                y = jnp.dot(Mm.astype(BF16), xh.astype(BF16), preferred_element_type=F32)
                y = y + lax.dot_general(Cb, hs.astype(BF16), (((1,), (1,)), ((), ())),
                                        preferred_element_type=F32) * jnp.exp(lcol)
                Ll = Lc_col[C - 1:C, h:h + 1]
                w = jnp.exp(Ll - lcol) * dt[:, h:h + 1]
                h_ref[h] = jnp.exp(Ll) * hs + lax.dot_general(
                    (w * xh).astype(BF16), Bb, (((0,), (0,)), ((), ())), preferred_element_type=F32)
                ys.append(y + dsk[:, h:h + 1] * xh)
        outs = []
        gw = GROUP_W // G
        for g in range(G):
            yg = jnp.concatenate(ys[g * HPG:(g + 1) * HPG], axis=1) * _silu(zc[:, g * gw:(g + 1) * gw])
            outs.append(_rms(yg, gn[:, g * gw:(g + 1) * gw]))
        y_ref[0, pl.ds(r0, C), :] = jnp.concatenate(outs, axis=1)
        return carry

    lax.fori_loop(0, T // C, chunk, 0)
    h1_ref[0] = h_ref[...]


def _ssd(z, xbc_raw, dt_raw, conv_buf, conv_w, conv_b, a_log, dt_bias, d_skip, gnorm, h0, C, t_real):
    B, T, _ = z.shape
    H = SSM_HEADS
    HP = dt_raw.shape[-1]

    def row(a):
        a = a.astype(F32).reshape(1, -1)
        return jnp.pad(a, ((0, 0), (0, HP - a.shape[1]))) if a.shape[1] == H else a

    tok = lambda w: pl.BlockSpec((1, T, w), lambda b: (b, 0, 0))
    cst = lambda r, w: pl.BlockSpec((r, w), lambda b: (0, 0))
    st = pl.BlockSpec((1, H, SSM_HEAD_DIM, SSM_STATE), lambda b: (b, 0, 0, 0))
    bufs = pl.BlockSpec((1, SSM_CONV - 1, SSM_XBC), lambda b: (b, 0, 0))
    return pl.pallas_call(
        functools.partial(_ssd_body, T=T, C=C, t_real=t_real),
        grid=(B,),
        in_specs=[tok(GROUP_W), tok(SSM_XBC), tok(HP), bufs, cst(SSM_CONV, SSM_XBC), cst(1, SSM_XBC),
                  cst(1, HP), cst(1, HP), cst(1, HP), cst(1, GROUP_W), st],
        out_specs=[tok(GROUP_W), st, bufs],
        out_shape=[jax.ShapeDtypeStruct((B, T, GROUP_W), F32),
                   jax.ShapeDtypeStruct((B, H, SSM_HEAD_DIM, SSM_STATE), F32),
                   jax.ShapeDtypeStruct((B, SSM_CONV - 1, SSM_XBC), F32)],
        scratch_shapes=[pltpu.VMEM((T + 2 * SUBLANE, SSM_XBC), F32),
                        pltpu.VMEM((H, SSM_HEAD_DIM, SSM_STATE), F32)],
        compiler_params=_cparams("parallel"),
        name="ssd",
    )(z, xbc_raw, dt_raw, conv_buf, conv_w.astype(F32), row(conv_b), row(-jnp.exp(a_log.astype(F32))),
      row(dt_bias), row(d_skip), row(gnorm), h0)


def _topk_mask(score, k, idx_bits):
    R, L = score.shape
    bits = lax.bitcast_convert_type(score, I32)
    key = bits ^ ((bits >> 31) & jnp.int32(0x7FFFFFFF))
    kf = jnp.float32(k)

    def count(m):
        return jnp.sum(m.astype(F32), axis=1, keepdims=True)

    int_min = jnp.int32(-2 ** 31)
    t0 = jnp.where(count(key >= 0) >= kf, jnp.int32(0), int_min)

    def vbody(i, t):
        cand = t | (jnp.int32(1) << (30 - i))
        return jnp.where(count(key >= cand) >= kf, cand, t)

    thr = lax.fori_loop(0, 31, vbody, t0)
    gt = key > thr
    tie = key == thr
    need = kf - count(gt)
    idx = lax.broadcasted_iota(I32, (1, L), 1)

    def ibody(i, c):
        cand = c | (jnp.int32(1) << (idx_bits - 1 - i))
        return jnp.where(count(tie & (idx < cand)) < need, cand, c)

    cut = lax.fori_loop(0, idx_bits, ibody, jnp.zeros((R, 1), I32))
    return gt | (tie & (idx <= cut))


def _dsa_prompt_body(q_ref, k_ref, v_ref, iq_ref, ik_ref, iw_ref, o_ref, *, tq, L, topk, idx_bits):
    qi = pl.program_id(1)
    nt = (((1,), (1,)), ((), ()))
    ikb = ik_ref[0]
    iw = iw_ref[0]
    acc = jnp.zeros((tq, L), F32)
    for h in range(IDX_HEADS):
        s = lax.dot_general(iq_ref[0, h], ikb, nt, preferred_element_type=F32)
        acc = acc + iw[:, h:h + 1] * jnp.maximum(s, 0.0)
    qpos = qi * tq + lax.broadcasted_iota(I32, (tq, 1), 0)
    kpos = lax.broadcasted_iota(I32, (1, L), 1)
    causal = kpos <= qpos
    score = jnp.where(causal, acc * IDX_SCALE, NEG_INF)
    sel = _topk_mask(score, topk, idx_bits) & causal
    kb = k_ref[0]
    vb = v_ref[0]
    for h in range(DSA_HEADS):
        s = lax.dot_general(q_ref[0, h], kb, nt, preferred_element_type=F32) * (DSA_DH ** -0.5)
        s = jnp.where(sel, s, NEG_INF)
        p = jnp.exp(s - jnp.max(s, axis=1, keepdims=True))
        l = jnp.sum(p, axis=1, keepdims=True)
        o = jnp.dot(p.astype(BF16), vb, preferred_element_type=F32)
        o_ref[0, :, h * DSA_DH:(h + 1) * DSA_DH] = o / l


def _dsa_prompt(q, k, v, iq, ik, iw, tq=128):
    B, H, T, D = q.shape
    topk = min(DSA_TOPK, T // 4)
    return pl.pallas_call(
        functools.partial(_dsa_prompt_body, tq=tq, L=T, topk=topk, idx_bits=max(1, (T - 1).bit_length())),
        grid=(B, T // tq),
        in_specs=[
            pl.BlockSpec((1, H, tq, D), lambda b, i: (b, 0, i, 0)),
            pl.BlockSpec((1, T, D), lambda b, i: (b, 0, 0)),
            pl.BlockSpec((1, T, D), lambda b, i: (b, 0, 0)),
            pl.BlockSpec((1, IDX_HEADS, tq, IDX_DIM), lambda b, i: (b, 0, i, 0)),
            pl.BlockSpec((1, T, IDX_DIM), lambda b, i: (b, 0, 0)),
            pl.BlockSpec((1, tq, IDX_HEADS), lambda b, i: (b, i, 0)),
        ],
        out_specs=pl.BlockSpec((1, tq, H * D), lambda b, i: (b, i, 0)),
        out_shape=jax.ShapeDtypeStruct((B, T, H * D), F32),
        compiler_params=_cparams("parallel", "parallel"),
        name="dsa_prompt",
    )(q, k, v, iq, ik, iw)


def _mqa_causal_body(q_ref, k_ref, v_ref, o_ref, *, tq, tk, H, scale):
    qi = pl.program_id(1)
    Dk = q_ref.shape[-1]
    Dv = v_ref.shape[-1]
    R = H * tq
    q = q_ref[0].reshape(R, Dk)
    qpos = qi * tq + lax.broadcasted_iota(I32, (R, 1), 0) % tq
    nkv = ((qi + 1) * tq + tk - 1) // tk

    def body(c, carry):
        m, l, acc = carry
        c0 = pl.multiple_of(c * tk, tk)
        kc = k_ref[0, pl.ds(c0, tk), :]
        vc = v_ref[0, pl.ds(c0, tk), :]
        s = lax.dot_general(q, kc, (((1,), (1,)), ((), ())), preferred_element_type=F32) * scale
        kpos = c0 + lax.broadcasted_iota(I32, (1, tk), 1)
        s = jnp.where(kpos <= qpos, s, NEG_INF)
        m_new = jnp.maximum(m, jnp.max(s, axis=1, keepdims=True))
        alpha = jnp.exp(m - m_new)
        p = jnp.exp(s - m_new)
        l = alpha * l + jnp.sum(p, axis=1, keepdims=True)
        acc = alpha * acc + jnp.dot(p.astype(BF16), vc, preferred_element_type=F32)
        return m_new, l, acc

    m0 = jnp.full((R, 1), NEG_INF, F32)
    m, l, acc = lax.fori_loop(0, nkv, body, (m0, jnp.zeros((R, 1), F32), jnp.zeros((R, Dv), F32)))
    o = acc / l
    for h in range(H):
        o_ref[0, :, h * Dv:(h + 1) * Dv] = o[h * tq:(h + 1) * tq]


def _mqa_causal(q, k, v, scale, tq=128, tk=512):
    B, H, T, Dk = q.shape
    Dv = v.shape[-1]
    tk = min(tk, T)
    return pl.pallas_call(
        functools.partial(_mqa_causal_body, tq=tq, tk=tk, H=H, scale=scale),
        grid=(B, T // tq),
        in_specs=[
            pl.BlockSpec((1, H, tq, Dk), lambda b, i: (b, 0, i, 0)),
            pl.BlockSpec((1, T, Dk), lambda b, i: (b, 0, 0)),
            pl.BlockSpec((1, T, Dv), lambda b, i: (b, 0, 0)),
        ],
        out_specs=pl.BlockSpec((1, tq, H * Dv), lambda b, i: (b, i, 0)),
        out_shape=jax.ShapeDtypeStruct((B, T, H * Dv), F32),
        compiler_params=_cparams("parallel", "parallel"),
        name="mqa_causal",
    )(q, k, v)


def _mem_attn_body(q_ref, k_ref, v_ref, o_ref):
    D = MEM_DH
    for h in range(MEM_HEADS):
        q = q_ref[0, :, h * D:(h + 1) * D].astype(BF16)
        kh = k_ref[0, :, h * D:(h + 1) * D].astype(BF16)
        vh = v_ref[0, :, h * D:(h + 1) * D].astype(BF16)
        s = lax.dot_general(q, kh, (((1,), (1,)), ((), ())), preferred_element_type=F32) * (D ** -0.5)
        p = jnp.exp(s - jnp.max(s, axis=1, keepdims=True))
        l = jnp.sum(p, axis=1, keepdims=True)
        o_ref[0, :, h * D:(h + 1) * D] = jnp.dot(p.astype(BF16), vh, preferred_element_type=F32) / l


def _mem_attn(q, mk, mv, tq_cap=512):
    B, T, W = q.shape
    tq = _pick_tile(T, tq_cap, SUBLANE)
    N = mk.shape[1]
    return pl.pallas_call(
        _mem_attn_body,
        grid=(B, T // tq),
        in_specs=[pl.BlockSpec((1, tq, W), lambda b, i: (b, i, 0)),
                  pl.BlockSpec((1, N, W), lambda b, i: (b, 0, 0)),
                  pl.BlockSpec((1, N, W), lambda b, i: (b, 0, 0))],
        out_specs=pl.BlockSpec((1, tq, W), lambda b, i: (b, i, 0)),
        out_shape=jax.ShapeDtypeStruct((B, T, W), F32),
        compiler_params=_cparams("parallel", "parallel"),
        name="mem_attn",
    )(q, mk, mv)


N_PAGED = 5


def _paged_body(pt_ref, iq_ref, iw_ref, dq_ref, mq_ref, ikn_ref, dkn_ref, dvn_ref, cn_ref, krn_ref,
                cik, ck, cv, cc, ckr, yb_ref, ol_ref, ik_buf, k_buf, v_buf, c_buf, kr_buf, sem,
                *, layer, n_pages, T, topk, idx_bits, chunk):
    b = pl.program_id(0)
    nb = pl.num_programs(0)
    slot = b % 2
    past = n_pages * PAGE_SIZE
    LT = past + LANE
    caches = (cik, ck, cv, cc, ckr)
    bufs = (ik_buf, k_buf, v_buf, c_buf, kr_buf)
    news = (ikn_ref, dkn_ref, dvn_ref, cn_ref, krn_ref)

    def page_copy(a, seq, sl, p):
        return pltpu.make_async_copy(caches[a].at[layer, pt_ref[seq, p]],
                                     bufs[a].at[sl, pl.ds(p * PAGE_SIZE, PAGE_SIZE)], sem.at[sl, a])

    def for_pages(seq, sl, start):
        def body(p, carry):
            for a in range(N_PAGED):
                cp = page_copy(a, seq, sl, p)
                cp.start() if start else cp.wait()
            return carry
        lax.fori_loop(0, n_pages, body, 0)

    @pl.when(b == 0)
    def _():
        for a in range(N_PAGED):
            bufs[a][:, past:LT, :] = jnp.zeros((2, LANE, bufs[a].shape[-1]), F32)
        for_pages(0, 0, True)

    @pl.when(b + 1 < nb)
    def _():
        for_pages(b + 1, 1 - slot, True)

    for_pages(b, slot, False)
    for a in range(N_PAGED):
        bufs[a][slot, past:past + SUBLANE, :] = news[a][0]

    nt = (((1,), (1,)), ((), ()))
    starts = list(range(0, LT, chunk))
    tpos = lax.broadcasted_iota(I32, (T, 1), 0)
    kpos = lax.broadcasted_iota(I32, (1, LT), 1)
    valid = (kpos < past) | ((kpos - past) <= tpos)

    iq = iq_ref[0]
    iw = iw_ref[0]
    parts = []
    for c0 in starts:
        ikc = ik_buf[slot, c0:c0 + chunk, :].astype(BF16)
        s = lax.dot_general(iq, ikc, nt, preferred_element_type=F32)
        s = jnp.maximum(s, 0.0) * iw
        parts.append(jnp.sum(s.reshape(T, IDX_HEADS, chunk), axis=1))
    score = jnp.where(valid, jnp.concatenate(parts, axis=1) * IDX_SCALE, NEG_INF)
    sel = _topk_mask(score, topk, idx_bits) & valid

    def attend(q_parts, key_bufs, val_buf, mask, scale, out_ref):
        bias = jnp.where(mask, 0.0, NEG_INF)
        bias = jnp.concatenate([bias] * (q_parts[0].shape[0] // T), axis=0)
        sp = []
        for c0 in starts:
            s = None
            for qp, kb in zip(q_parts, key_bufs):
                t = lax.dot_general(qp, kb[slot, c0:c0 + chunk, :].astype(BF16), nt,
                                    preferred_element_type=F32)
                s = t if s is None else s + t
            sp.append(s)
        s = jnp.concatenate(sp, axis=1) * scale + bias
        p = jnp.exp(s - jnp.max(s, axis=1, keepdims=True))
        l = jnp.sum(p, axis=1, keepdims=True)
        o = None
        for c0 in starts:
            t = jnp.dot(p[:, c0:c0 + chunk].astype(BF16), val_buf[slot, c0:c0 + chunk, :].astype(BF16),
                        preferred_element_type=F32)
            o = t if o is None else o + t
        out_ref[0] = o / l

    attend([dq_ref[0]], [k_buf], v_buf, sel, DSA_DH ** -0.5, yb_ref)
    mq = mq_ref[0]
    attend([mq[:, :MLA_KVLORA], mq[:, MLA_KVLORA:MLA_KVLORA + MLA_ROPE]], [c_buf, kr_buf], c_buf,
           valid, MLA_SCALE, ol_ref)


def _paged_attn(page_table, iq, iw, dq, mq, ikn, dkn, dvn, cn, krn, cik, ck, cv, cc, ckr, layer, T):
    B, n_pages = page_table.shape
    past = n_pages * PAGE_SIZE
    LT = past + LANE
    topk = min(DSA_TOPK, (past + T) // 4)
    nchunk = 1
    for cand in (5, 4, 3, 2):
        if (LT // LANE) % cand == 0 and LT // cand >= 512:
            nchunk = cand
            break
    per_seq = lambda a: pl.BlockSpec((1,) + a.shape[1:], lambda b, pt: (b, 0, 0))
    anyspec = pl.BlockSpec(memory_space=pl.ANY)
    vm_in = [iq, iw, dq, mq, ikn, dkn, dvn, cn, krn]
    grid_spec = pltpu.PrefetchScalarGridSpec(
        num_scalar_prefetch=1,
        grid=(B,),
        in_specs=[per_seq(a) for a in vm_in] + [anyspec] * N_PAGED,
        out_specs=[pl.BlockSpec((1, dq.shape[1], DSA_DH), lambda b, pt: (b, 0, 0)),
                   pl.BlockSpec((1, mq.shape[1], MLA_KVLORA), lambda b, pt: (b, 0, 0))],
        scratch_shapes=[pltpu.VMEM((2, LT, c.shape[-1]), F32) for c in (cik, ck, cv, cc, ckr)]
        + [pltpu.SemaphoreType.DMA((2, N_PAGED))],
    )
    return pl.pallas_call(
        functools.partial(_paged_body, layer=layer, n_pages=n_pages, T=T, topk=topk,
                          idx_bits=max(1, (LT - 1).bit_length()), chunk=LT // nchunk),
        grid_spec=grid_spec,
        out_shape=[jax.ShapeDtypeStruct((B, dq.shape[1], DSA_DH), F32),
                   jax.ShapeDtypeStruct((B, mq.shape[1], MLA_KVLORA), F32)],
        compiler_params=_cparams("arbitrary"),
        name="paged_attn",
    )(page_table, *vm_in, cik, ck, cv, cc, ckr)


def _in_layout():
    names = ("hq", "hf", "hi", "hg", "dq", "dk", "dv", "iq", "ik", "iw", "sz", "sxbc", "sdt", "mcq", "mckv", "mkr")
    src, o = {}, 0
    for n, w in zip(names, IN_WIDTHS):
        src[n] = (o, w)
        o += w
    groups = (("hq",), ("hf",), ("hi",), ("hg",), ("dq",), ("dk",), ("dv",), ("iq",), ("ik", "iw"),
              ("sz",), ("sxbc",), ("sdt",), ("mcq",), ("mckv",), ("mkr",))
    pieces, dst, o = [], {}, 0
    for g in groups:
        w = 0
        for n in g:
            dst[n] = (o + w, src[n][1])
            pieces.append(("copy",) + src[n])
            w += src[n][1]
        pad = (-w) % LANE
        if pad:
            pieces.append(("zero", 0, pad))
        o += w + pad
    return pieces, dst, o


_IN_PIECES, _IN_DST, D_IN_PAD = _in_layout()


def _pad_w_in(w):
    cols = []
    for kind, o, n in _IN_PIECES:
        cols.append(w[:, o:o + n] if kind == "copy" else jnp.zeros((w.shape[0], n), w.dtype))
    return jnp.concatenate(cols, axis=1).astype(BF16)


def _part(u, name):
    o, w = _IN_DST[name]
    return u[..., o:o + w]


def _rope_cos_sin(pos, rot_dim):
    inv = ROPE_THETA ** (-jnp.arange(0, rot_dim, 2, dtype=F32) / rot_dim)
    ang = pos.astype(F32)[:, None] * inv[None, :]
    return jnp.cos(ang), jnp.sin(ang)


def _rotate(x, cos, sin):
    r2 = cos.shape[-1]
    x1, x2 = x[..., :r2], x[..., r2:2 * r2]
    c, s = cos[:, None, :], sin[:, None, :]
    return jnp.concatenate([x1 * c - x2 * s, x1 * s + x2 * c], axis=-1)


def _partial_rotary(x, cos, sin):
    r = 2 * cos.shape[-1]
    return jnp.concatenate([_rotate(x[..., :r], cos, sin), x[..., r:]], axis=-1)


def _pad_rows(a, rows):
    return jnp.pad(a, ((0, 0), (0, rows - a.shape[1])) + ((0, 0),) * (a.ndim - 2))


def _heads_first(a):
    return jnp.swapaxes(a, 1, 2)


def kernel(x_prompt, x_sample, cache_dsa_k, cache_dsa_v, cache_dsa_ik, cache_mla_c, cache_mla_kr, cache_mem_k, cache_mem_v, state_hgrn, state_ssm, state_ssm_conv, state_ffn_conv, page_table, mem_prompt, w_in, w_out, mix_norm, hgrn_lb, hgrn_norm, mla_q_norm, mla_kv_norm, mla_w_uq, mla_w_uk, mla_w_uv, ssm_conv_w, ssm_conv_b, ssm_a_log, ssm_dt_bias, ssm_d, ssm_norm, mem_norm, cross_norm, cross_wq, cross_wk, cross_wv, cross_wo, ffn_norm, ffn_w_up, ffn_conv_w, ffn_conv_b, ffn_w_down, final_norm):
    Bp, Tp, D = x_prompt.shape
    Bs, Ts, _ = x_sample.shape
    Mp, Ms = Bp * Tp, Bs * Ts
    depth = w_in.shape[0]
    past = page_table.shape[1] * PAGE_SIZE
    n_mem = mem_prompt.shape[1]
    TS_PAD = 16

    lb = jnp.cumsum(jax.nn.softmax(hgrn_lb.astype(F32), axis=0), axis=0)
    lb = lb - lb[:1]

    pos_p = jnp.arange(Tp, dtype=jnp.int32)
    pos_s = past + jnp.arange(Ts, dtype=jnp.int32)
    rope = {}
    for name, pos in (("p", pos_p), ("s", pos_s)):
        rope[name] = (_rope_cos_sin(pos, DSA_DH // 4), _rope_cos_sin(pos, IDX_DIM // 4), _rope_cos_sin(pos, MLA_ROPE))

    x = jnp.concatenate([x_prompt.reshape(Mp, D), x_sample.reshape(Ms, D)], axis=0)
    p_rows, s_rows, p_mk, p_mv = [], [], [], []

    def split(a):
        return a[:Mp].reshape(Bp, Tp, -1), a[Mp:].reshape(Bs, Ts, -1)

    for l in range(depth):
        w_in_l = _pad_w_in(w_in[l])
        u = _mm(x, w_in_l, gain=mix_norm[l], name="mm_in")

        q_d = _mm(_part(u, "mcq"), mla_w_uq[l].astype(BF16), gain=mla_q_norm[l], name="mm_uq")
        uk = jnp.pad(jnp.transpose(mla_w_uk[l], (1, 2, 0)), ((0, 0), (0, MLA_ROPE), (0, 0)))
        uk_bd = jax.scipy.linalg.block_diag(*[uk[h] for h in range(MLA_HEADS)]).astype(BF16)
        uv_bd = jax.scipy.linalg.block_diag(*[mla_w_uv[l][:, h, :] for h in range(MLA_HEADS)]).astype(BF16)
        q_lat = _mm(q_d, uk_bd, name="mm_uk")
        c_kv = _rmsnorm(_part(u, "mckv"), mla_kv_norm[l])

        ys, olats, rows_by_group = [], [], []
        for gi, (ug, qd_g, ql_g, ckv_g) in enumerate(zip(split(u), split(q_d), split(q_lat), split(c_kv))):
            prompt = gi == 0
            B, T = (Bp, Tp) if prompt else (Bs, Ts)
            (cos_d, sin_d), (cos_i, sin_i), (cos_m, sin_m) = rope["p" if prompt else "s"]
            hq, hf, hi, hg = (_part(ug, n) for n in ("hq", "hf", "hi", "hg"))
            sz, sxbc, sdt = _part(ug, "sz"), _part(ug, "sxbc"), ug[..., _IN_DST["sdt"][0]:_IN_DST["sdt"][0] + LANE]
            dv = _part(ug, "dv")
            q_b = _partial_rotary(_part(ug, "dq").reshape(B, T, DSA_HEADS, DSA_DH), cos_d, sin_d)
            k_b = _partial_rotary(_part(ug, "dk").reshape(B, T, 1, DSA_DH), cos_d, sin_d)[:, :, 0]
            iq_b = _partial_rotary(_part(ug, "iq").reshape(B, T, IDX_HEADS, IDX_DIM), cos_i, sin_i)
            ik_b = _partial_rotary(_part(ug, "ik").reshape(B, T, 1, IDX_DIM), cos_i, sin_i)[:, :, 0]
            iw = _part(ug, "iw")
            q_rope = _rotate(qd_g.reshape(B, T, MLA_HEADS, MLA_NOPE + MLA_ROPE)[..., MLA_NOPE:], cos_m, sin_m)
            k_rope = _rotate(_part(ug, "mkr").reshape(B, T, 1, MLA_ROPE), cos_m, sin_m)[:, :, 0]
            q_aug = jnp.concatenate([ql_g.reshape(B, T, MLA_HEADS, MLA_KVLORA), q_rope,
                                     jnp.zeros((B, T, MLA_HEADS, LANE - MLA_ROPE), F32)], axis=-1)
            q_aug = _heads_first(q_aug).astype(BF16)

            if prompt:
                y_a, hg1 = _hgrn(hq, hf, hi, hg, lb[l], hgrn_norm[l],
                                 jnp.zeros((B, HG_HEADS, HG_DK, HG_DK), F32), HG_CHUNK, T)
                y_c, ssm1, sconv1 = _ssd(sz, sxbc, sdt, jnp.zeros((B, SSM_CONV - 1, SSM_XBC), F32),
                                         ssm_conv_w[l], ssm_conv_b[l], ssm_a_log[l], ssm_dt_bias[l], ssm_d[l],
                                         ssm_norm[l], jnp.zeros((B, SSM_HEADS, SSM_HEAD_DIM, SSM_STATE), F32),
                                         min(SSM_CHUNK, T), T)
                y_b = _dsa_prompt(_heads_first(q_b).astype(BF16), k_b.astype(BF16), dv.astype(BF16),
                                  _heads_first(iq_b).astype(BF16), ik_b.astype(BF16), iw)
                k_aug = jnp.concatenate([ckv_g, k_rope, jnp.zeros((B, T, LANE - MLA_ROPE), F32)], axis=-1)
                o_lat = _mqa_causal(q_aug, k_aug.astype(BF16), ckv_g.astype(BF16), MLA_SCALE)
            else:
                pt = lambda a: _pad_rows(a, TS_PAD)
                y_a, hg1 = _hgrn(pt(hq), pt(hf), pt(hi), pt(hg), lb[l], hgrn_norm[l], state_hgrn[l], HG_CHUNK, T)
                y_c, ssm1, sconv1 = _ssd(pt(sz), pt(sxbc), pt(sdt), state_ssm_conv[l], ssm_conv_w[l],
                                         ssm_conv_b[l], ssm_a_log[l], ssm_dt_bias[l], ssm_d[l], ssm_norm[l],
                                         state_ssm[l], TS_PAD, T)
                y_a, y_c = y_a[:, :T], y_c[:, :T]
                pr = lambda a: _pad_rows(a, SUBLANE)
                yb, ol = _paged_attn(
                    page_table,
                    iq_b.reshape(B, T * IDX_HEADS, IDX_DIM).astype(BF16), iw.reshape(B, T * IDX_HEADS, 1),
                    _heads_first(q_b).reshape(B, DSA_HEADS * T, DSA_DH).astype(BF16),
                    q_aug.reshape(B, MLA_HEADS * T, 2 * LANE),
                    pr(ik_b), pr(k_b), pr(dv), pr(ckv_g), pr(k_rope),
                    cache_dsa_ik, cache_dsa_k, cache_dsa_v, cache_mla_c, cache_mla_kr, l, T)
                y_b = _heads_first(yb.reshape(B, DSA_HEADS, T, DSA_DH)).reshape(B, T, GROUP_W)
                o_lat = _heads_first(ol.reshape(B, MLA_HEADS, T, MLA_KVLORA)).reshape(B, T, MLA_HEADS * MLA_KVLORA)
            ys.append((y_a, y_b, y_c))
            olats.append(o_lat.reshape(B * T, -1))
            rows_by_group.append([k_b, dv, ik_b, ckv_g, k_rope, hg1, ssm1, sconv1])

        y_d = _mm(jnp.concatenate(olats, axis=0), uv_bd, name="mm_uv")
        y_abc = [jnp.concatenate([g[i].reshape(-1, GROUP_W) for g in ys], axis=0) for i in range(3)]
        x = _mm(jnp.concatenate(y_abc + [y_d], axis=1), w_out[l].astype(BF16), res=x, name="mm_out")

        qc_p, qc_s = split(_mm(x, cross_wq[l].astype(BF16), gain=cross_norm[l], name="mm_cq"))
        mem = mem_prompt.reshape(Bp * n_mem, D)
        mk = _mm(mem, cross_wk[l].astype(BF16), gain=mem_norm[l], name="mm_ck").reshape(Bp, n_mem, MEM_HEADS * MEM_DH)
        mv = _mm(mem, cross_wv[l].astype(BF16), gain=mem_norm[l], name="mm_cv").reshape(Bp, n_mem, MEM_HEADS * MEM_DH)
        att_p = _mem_attn(qc_p, mk, mv)
        att_s = _mem_attn(qc_s, cache_mem_k[l].reshape(Bs, -1, MEM_HEADS * MEM_DH),
                          cache_mem_v[l].reshape(Bs, -1, MEM_HEADS * MEM_DH))
        att = jnp.concatenate([att_p.reshape(Mp, -1), att_s.reshape(Ms, -1)], axis=0)
        x = _mm(att, cross_wo[l].astype(BF16), res=x, name="mm_co")
        p_mk.append(mk.reshape(Bp, n_mem, MEM_HEADS, MEM_DH))
        p_mv.append(mv.reshape(Bp, n_mem, MEM_HEADS, MEM_DH))

        w_up = ffn_w_up[l].astype(BF16)
        h_p, fconv_p = _ffn_up(x[:Mp], ffn_norm[l], w_up, ffn_conv_w[l], ffn_conv_b[l],
                               jnp.zeros((Bp, FFN_CONV - 1, D_FF), F32), Tp)
        prev_s = jnp.broadcast_to(jnp.swapaxes(state_ffn_conv[l], 0, 1)[:, :, None, :],
                                  (FFN_CONV - 1, Bs, Ts, D_FF)).reshape(FFN_CONV - 1, Ms, D_FF)
        h_s, a_s = _ffn_up(x[Mp:], ffn_norm[l], w_up, ffn_conv_w[l], ffn_conv_b[l], prev_s, Ts)
        fconv_s = a_s.reshape(Bs, Ts, D_FF)[:, Ts - (FFN_CONV - 1):]
        x = _mm(jnp.concatenate([h_p, h_s], axis=0), ffn_w_down[l].astype(BF16), res=x, tn_cap=512,
                name="mm_down")

        p_rows.append(rows_by_group[0] + [fconv_p])
        s_rows.append(rows_by_group[1] + [fconv_s])

    y = _rmsnorm(x, final_norm)
    y_prompt, y_sample = y[:Mp].reshape(Bp, Tp, D), y[Mp:].reshape(Bs, Ts, D)
    stack = lambda rows, i: jnp.stack([r[i] for r in rows], axis=0)
    return ((y_prompt, y_sample)
            + tuple(stack(p_rows, i) for i in range(9))
            + (jnp.stack(p_mk, axis=0), jnp.stack(p_mv, axis=0))
            + tuple(stack(s_rows, i) for i in range(9)))
```

```python
import functools
import math

import jax
import jax.numpy as jnp
from jax import lax
from jax.experimental import pallas as pl
from jax.experimental.pallas import tpu as pltpu

F32 = jnp.float32
BF16 = jnp.bfloat16
I32 = jnp.int32

LANE = 128
SUBLANE = 8
VMEM_LIMIT = 56 * 1024 * 1024

D_MODEL = 2048
DEPTH = 4
PAGE_SIZE = 128
GROUP_W = D_MODEL // 4
HG_HEADS = 4
HG_DK = GROUP_W // HG_HEADS
HG_CHUNK = 16
DSA_HEADS = 4
DSA_DH = GROUP_W // DSA_HEADS
IDX_HEADS = 8
IDX_DIM = 64
DSA_TOPK = 256
IDX_SCALE = (IDX_DIM * IDX_HEADS) ** -0.5
SSM_HEAD_DIM = 64
SSM_HEADS = GROUP_W // SSM_HEAD_DIM
SSM_GROUPS = 2
SSM_HPG = SSM_HEADS // SSM_GROUPS
SSM_STATE = 128
SSM_CONV = 4
SSM_XBC = GROUP_W + 2 * SSM_GROUPS * SSM_STATE
SSM_CHUNK = 128
MLA_HEADS = 4
MLA_V = GROUP_W // MLA_HEADS
MLA_NOPE = 96
MLA_ROPE = 32
MLA_QLORA = 384
MLA_KVLORA = 128
MLA_SCALE = (MLA_NOPE + MLA_ROPE) ** -0.5
N_MEM = 256
MEM_HEADS = 4
MEM_DH = 128
D_FF = (D_MODEL * 11) // 4
FFN_CONV = 3
ROPE_THETA = 500000.0
EPS = 1e-6
IN_WIDTHS = (GROUP_W, GROUP_W, GROUP_W, GROUP_W,
             DSA_HEADS * DSA_DH, DSA_DH, DSA_DH,
             IDX_HEADS * IDX_DIM, IDX_DIM, IDX_HEADS,
             GROUP_W, SSM_XBC, SSM_HEADS,
             MLA_QLORA, MLA_KVLORA, MLA_ROPE)
NEG_INF = float("-inf")


def _cparams(*sem):
    return pltpu.CompilerParams(dimension_semantics=sem, vmem_limit_bytes=VMEM_LIMIT)


def _pick_tile(n, cap, unit=LANE):
    if n <= cap:
        return n
    best = None
    for t in range(unit, cap + 1, unit):
        if n % t == 0:
            best = t
    assert best is not None, (n, cap)
    return best


def _silu(x):
    return x * (1.0 / (1.0 + jnp.exp(-x)))


def _softplus(x):
    return jnp.maximum(x, 0.0) + jnp.log(1.0 + jnp.exp(-jnp.abs(x)))


def _log_sigmoid(x):
    return jnp.minimum(x, 0.0) - jnp.log(1.0 + jnp.exp(-jnp.abs(x)))


def _rms(x, g):
    return x * lax.rsqrt(jnp.mean(x * x, axis=-1, keepdims=True) + EPS) * g


def _mm_body(*refs, norm, has_res, stage):
    refs = list(refs)
    x_ref = refs.pop(0)
    g_ref = refs.pop(0) if norm else None
    w_ref = refs.pop(0)
    res_ref = refs.pop(0) if has_res else None
    o_ref = refs.pop(0)
    if stage:
        xb_ref = refs.pop(0)

        @pl.when(pl.program_id(1) == 0)
        def _():
            x = x_ref[...].astype(F32)
            if norm:
                x = _rms(x, g_ref[...])
            xb_ref[...] = x.astype(BF16)

        xb = xb_ref[...]
    else:
        xb = x_ref[...]
    acc = jnp.dot(xb, w_ref[...], preferred_element_type=F32)
    if has_res:
        acc = acc + res_ref[...]
    o_ref[...] = acc.astype(o_ref.dtype)


def _mm(x, w, gain=None, res=None, out_dtype=F32, tm_cap=1088, tn_cap=640, name="mm"):
    M, K = x.shape
    N = w.shape[1]
    tm = _pick_tile(M, tm_cap, SUBLANE)
    tn = _pick_tile(N, tn_cap)
    norm = gain is not None
    stage = norm or x.dtype != BF16
    in_specs = [pl.BlockSpec((tm, K), lambda i, j: (i, 0))]
    args = [x]
    if norm:
        in_specs.append(pl.BlockSpec((1, K), lambda i, j: (0, 0)))
        args.append(gain.reshape(1, K).astype(F32))
    in_specs.append(pl.BlockSpec((K, tn), lambda i, j: (0, j)))
    args.append(w)
    if res is not None:
        in_specs.append(pl.BlockSpec((tm, tn), lambda i, j: (i, j)))
        args.append(res)
    return pl.pallas_call(
        functools.partial(_mm_body, norm=norm, has_res=res is not None, stage=stage),
        grid=(M // tm, N // tn),
        in_specs=in_specs,
        out_specs=pl.BlockSpec((tm, tn), lambda i, j: (i, j)),
        out_shape=jax.ShapeDtypeStruct((M, N), out_dtype),
        scratch_shapes=[pltpu.VMEM((tm, K), BF16)] if stage else [],
        compiler_params=_cparams("parallel", "arbitrary"),
        name=name,
    )(*args)


def _norm_body(x_ref, g_ref, o_ref):
    o_ref[...] = _rms(x_ref[...], g_ref[...])


def _rmsnorm(x, gain, tm_cap=512):
    M, K = x.shape
    tm = _pick_tile(M, tm_cap, SUBLANE)
    return pl.pallas_call(
        _norm_body,
        grid=(M // tm,),
        in_specs=[pl.BlockSpec((tm, K), lambda i: (i, 0)), pl.BlockSpec((1, K), lambda i: (0, 0))],
        out_specs=pl.BlockSpec((tm, K), lambda i: (i, 0)),
        out_shape=jax.ShapeDtypeStruct((M, K), F32),
        compiler_params=_cparams("parallel"),
        name="rmsnorm",
    )(x, gain.reshape(1, K).astype(F32))


def _ffn_up_body(x_ref, g_ref, wa_ref, wb_ref, cw_ref, cb_ref, prev_ref, h_ref, tail_ref,
                 xb_ref, halo_ref, *, seq, tm, tn):
    i = pl.program_id(0)
    j = pl.program_id(1)

    @pl.when(j == 0)
    def _():
        xb_ref[...] = _rms(x_ref[...], g_ref[...]).astype(BF16)

    xb = xb_ref[...]
    a = jnp.dot(xb, wa_ref[...], preferred_element_type=F32)
    b = jnp.dot(xb, wb_ref[...], preferred_element_type=F32)
    row = lax.broadcasted_iota(I32, (tm, 1), 0)
    if seq >= tm:
        col = pl.multiple_of(j * tn, tn)

        @pl.when((i % (seq // tm)) == 0)
        def _():
            halo_ref[:, pl.ds(col, tn)] = prev_ref[0]

        halo = halo_ref[:, pl.ds(col, tn)]
        p1 = jnp.where(row == 0, halo[1:2], pltpu.roll(a, 1, 0))
        p2 = jnp.where(row == 0, halo[0:1], jnp.where(row == 1, halo[1:2], pltpu.roll(a, 2, 0)))
        halo_ref[:, pl.ds(col, tn)] = a[tm - 2:tm]
        tail_ref[0, :, pl.ds(col, tn)] = a[tm - 2:tm]
    else:
        t = row % seq
        prev = prev_ref[...]
        p1 = jnp.where(t == 0, prev[1], pltpu.roll(a, 1, 0))
        p2 = jnp.where(t == 0, prev[0], jnp.where(t == 1, prev[1], pltpu.roll(a, 2, 0)))
        tail_ref[...] = a
    cw = cw_ref[...]
    conv = cb_ref[...] + cw[0:1] * p2 + cw[1:2] * p1 + cw[2:3] * a
    h_ref[...] = (_silu(conv) * b).astype(h_ref.dtype)


def _ffn_up(x, gain, w_up, conv_w, conv_b, prev, seq, row0, nrows, tm_cap=1024, tn_cap=512):
    M, D = x.shape
    Fh = w_up.shape[1] // 2
    tm = _pick_tile(nrows, tm_cap, SUBLANE)
    tn = _pick_tile(Fh, tn_cap)
    nj = Fh // tn
    assert row0 % tm == 0
    r0 = row0 // tm
    big = seq >= tm
    if big:
        assert seq % tm == 0
        per = seq // tm
        prev_spec = pl.BlockSpec((1, 2, tn), lambda i, j: (i // per, 0, j))
        tail_spec = pl.BlockSpec((1, 2, Fh), lambda i, j: (i // per, 0, 0))
        tail_shape = jax.ShapeDtypeStruct((nrows // seq, 2, Fh), F32)
    else:
        assert tm % seq == 0
        prev_spec = pl.BlockSpec((2, tm, tn), lambda i, j: (0, i, j))
        tail_spec = pl.BlockSpec((tm, tn), lambda i, j: (i, j))
        tail_shape = jax.ShapeDtypeStruct((nrows, Fh), F32)
    in_specs = [
        pl.BlockSpec((tm, D), lambda i, j: (r0 + i, 0)),
        pl.BlockSpec((1, D), lambda i, j: (0, 0)),
        pl.BlockSpec((D, tn), lambda i, j: (0, j)),
        pl.BlockSpec((D, tn), lambda i, j: (0, j + nj)),
        pl.BlockSpec((FFN_CONV, tn), lambda i, j: (0, j)),
        pl.BlockSpec((1, tn), lambda i, j: (0, j)),
        prev_spec,
    ]
    args = [x, gain.reshape(1, D).astype(F32), w_up, w_up, conv_w.astype(F32),
            conv_b.reshape(1, Fh).astype(F32), prev]
    return pl.pallas_call(
        functools.partial(_ffn_up_body, seq=seq, tm=tm, tn=tn),
        grid=(nrows // tm, nj),
        in_specs=in_specs,
        out_specs=[pl.BlockSpec((tm, tn), lambda i, j: (i, j)), tail_spec],
        out_shape=[jax.ShapeDtypeStruct((nrows, Fh), BF16), tail_shape],
        scratch_shapes=[pltpu.VMEM((tm, D), BF16), pltpu.VMEM((2, Fh), F32)],
        compiler_params=_cparams("arbitrary", "arbitrary"),
        name="ffn_up",
    )(*args)


def _hgrn_body(q_ref, f_ref, i_ref, g_ref, loglb_ref, log1mlb_ref, onemlb_ref, gn_ref, s0_ref,
               y_ref, s1_ref, st_ref, *, T, C, t_real, nb):
    dk = HG_DK
    streams = [(b, h) for b in range(nb) for h in range(HG_HEADS)]
    for n, (b, h) in enumerate(streams):
        st_ref[n] = s0_ref[0, b, h].T
    row = lax.broadcasted_iota(I32, (C, 1), 0)
    tri = (lax.broadcasted_iota(I32, (C, C, 1), 0) >= lax.broadcasted_iota(I32, (C, C, 1), 1))

    def chunk(c, carry):
        r0 = pl.multiple_of(c * C, C)
        for n, (b, h) in enumerate(streams):
            cols = slice(h * dk, (h + 1) * dk)
            q = _silu(q_ref[b, pl.ds(r0, C), cols])
            z = f_ref[b, pl.ds(r0, C), cols]
            v = i_ref[b, pl.ds(r0, C), cols]
            loglb = loglb_ref[:, cols]
            b_ = log1mlb_ref[:, cols] + _log_sigmoid(z)
            logf = jnp.maximum(loglb, b_) + jnp.log(1.0 + jnp.exp(-jnp.abs(loglb - b_)))
            e = jnp.exp(-jnp.abs(z))
            k = onemlb_ref[:, cols] * (jnp.where(z >= 0, e, 1.0) / (1.0 + e))
            if t_real < T:
                live = (r0 + row) < t_real
                logf = jnp.where(live, logf, 0.0)
                k = jnp.where(live, k, 0.0)
            Lc = logf
            sh = 1
            while sh < C:
                Lc = Lc + jnp.where(row >= sh, pltpu.roll(Lc, sh, 0), 0.0)
                sh *= 2
            decay = jnp.exp(jnp.where(tri, Lc[:, None, :] - Lc[None, :, :], NEG_INF))
            A = jnp.sum(q[:, None, :] * k[None, :, :] * decay, axis=-1)
            St = st_ref[n]
            o = jnp.dot(A.astype(BF16), v.astype(BF16), preferred_element_type=F32)
            o = o + lax.dot_general((q * jnp.exp(Lc)).astype(BF16), St.astype(BF16),
                                    (((1,), (1,)), ((), ())), preferred_element_type=F32)
            Ll = Lc[C - 1:C, :]
            kd = k * jnp.exp(Ll - Lc)
            st_ref[n] = St * jnp.exp(Ll) + lax.dot_general(
                v.astype(BF16), kd.astype(BF16), (((0,), (0,)), ((), ())), preferred_element_type=F32)
            y_ref[b, pl.ds(r0, C), cols] = _rms(o, gn_ref[:, cols]) * _silu(g_ref[b, pl.ds(r0, C), cols])
        return carry

    lax.fori_loop(0, T // C, chunk, 0)
    for n, (b, h) in enumerate(streams):
        s1_ref[b, h] = st_ref[n].T


def _hgrn(hq, hf, hi, hg, lb, gnorm, S0, layer, C, t_real, nb):
    B, T, _ = hq.shape
    dk = HG_DK
    lb = lb.astype(F32).reshape(1, GROUP_W)
    consts = [jnp.log(lb), jnp.log1p(-lb), 1.0 - lb, gnorm.astype(F32).reshape(1, GROUP_W)]
    tok = pl.BlockSpec((nb, T, GROUP_W), lambda i: (i, 0, 0))
    cst = pl.BlockSpec((1, GROUP_W), lambda i: (0, 0))
    return pl.pallas_call(
        functools.partial(_hgrn_body, T=T, C=C, t_real=t_real, nb=nb),
        grid=(B // nb,),
        in_specs=[tok, tok, tok, tok, cst, cst, cst, cst,
                  pl.BlockSpec((1, nb, HG_HEADS, dk, dk), lambda i: (layer, i, 0, 0, 0))],
        out_specs=[tok, pl.BlockSpec((nb, HG_HEADS, dk, dk), lambda i: (i, 0, 0, 0))],
        out_shape=[jax.ShapeDtypeStruct((B, T, GROUP_W), F32),
                   jax.ShapeDtypeStruct((B, HG_HEADS, dk, dk), F32)],
        scratch_shapes=[pltpu.VMEM((nb * HG_HEADS, dk, dk), F32)],
        compiler_params=_cparams("parallel"),
        name="hgrn",
    )(hq, hf, hi, hg, *consts, S0)


def _split3(a):
    a1 = a.astype(BF16)
    r = a - a1.astype(F32)
    a2 = r.astype(BF16)
    a3 = (r - a2.astype(F32)).astype(BF16)
    return a1, a2, a3


def _dot3(m01, a, dims):
    out = None
    for part in _split3(a):
        lhs, rhs = (m01, part) if dims == "mn" else (part, m01)
        dn = (((1,), (0,)), ((), ())) if dims == "mn" else (((0,), (0,)), ((), ()))
        t = lax.dot_general(lhs, rhs, dn, preferred_element_type=F32)
        out = t if out is None else out + t
    return out


def _ssd_body(z_ref, x_ref, dt_ref, buf_ref, cw_ref, cb_ref, A_ref, dtb_ref, dsk_ref, gn_ref, h0_ref,
              y_ref, h1_ref, nbuf_ref, xp_ref, h_ref, *, T, C, t_real):
    P, N, G, HPG = SSM_HEAD_DIM, SSM_STATE, SSM_GROUPS, SSM_HPG
    H = SSM_HEADS
    PAD = SUBLANE
    xp_ref[0:PAD, :] = jnp.zeros((PAD, SSM_XBC), F32)
    xp_ref[PAD - (SSM_CONV - 1):PAD, :] = buf_ref[0, 0]
    xp_ref[PAD:PAD + T, :] = x_ref[0]
    nbuf_ref[0] = xp_ref[PAD + t_real - (SSM_CONV - 1):PAD + t_real, :]
    h_ref[...] = h0_ref[0, 0]
    cw = cw_ref[...]
    cb = cb_ref[...]
    Ah = A_ref[...]
    dtb = dtb_ref[...]
    dsk = dsk_ref[...]
    gn = gn_ref[...]
    r_i = lax.broadcasted_iota(I32, (C, C), 0)
    c_i = lax.broadcasted_iota(I32, (C, C), 1)
    tril = r_i >= c_i
    tril_b = tril.astype(BF16)
    triu_b = (r_i <= c_i).astype(BF16)
    eye_b = (r_i == c_i).astype(BF16)
    rowv = lax.broadcasted_iota(I32, (C, 1), 0)

    def chunk(c, carry):
        r0 = pl.multiple_of(c * C, C)
        X = xp_ref[pl.ds(r0, C + PAD), :]
        conv = cb + cw[3:4] * X[PAD:PAD + C] + cw[2:3] * X[PAD - 1:PAD - 1 + C] \
            + cw[1:2] * X[PAD - 2:PAD - 2 + C] + cw[0:1] * X[PAD - 3:PAD - 3 + C]
        xbc = _silu(conv)
        dt = _softplus(dt_ref[0, pl.ds(r0, C), :] + dtb)
        if t_real < T:
            dt = jnp.where((r0 + rowv) < t_real, dt, 0.0)
        a = dt * Ah
        Lc_col = _dot3(tril_b, a, "mn")
        Lc_row = _dot3(triu_b, a, "tn")
        dt_row = _dot3(eye_b, dt, "tn")
        zc = z_ref[0, pl.ds(r0, C), :]
        ys = []
        for g in range(G):
            Bm = xbc[:, GROUP_W + g * N:GROUP_W + (g + 1) * N]
            Cm = xbc[:, GROUP_W + G * N + g * N:GROUP_W + G * N + (g + 1) * N]
            Bb = Bm.astype(BF16)
            Cb = Cm.astype(BF16)
            CB = lax.dot_general(Cb, Bb, (((1,), (1,)), ((), ())), preferred_element_type=F32)
            for hh in range(HPG):
                h = g * HPG + hh
                xh = xbc[:, h * P:(h + 1) * P]
                lcol = Lc_col[:, h:h + 1]
                lrow = Lc_row[h:h + 1, :]
                Lmat = jnp.exp(jnp.where(tril, lcol - lrow, NEG_INF))
                Mm = CB * Lmat * dt_row[h:h + 1, :]
                hs = h_ref[h]
                y = jnp.dot(Mm.astype(BF16), xh.astype(BF16), preferred_element_type=F32)
                y = y + lax.dot_general(Cb, hs.astype(BF16), (((1,), (1,)), ((), ())),
                                        preferred_element_type=F32) * jnp.exp(lcol)
                Ll = Lc_col[C - 1:C, h:h + 1]
                w = jnp.exp(Ll - lcol) * dt[:, h:h + 1]
                h_ref[h] = jnp.exp(Ll) * hs + lax.dot_general(
                    (w * xh).astype(BF16), Bb, (((0,), (0,)), ((), ())), preferred_element_type=F32)
                ys.append(y + dsk[:, h:h + 1] * xh)
        outs = []
        gw = GROUP_W // G
        for g in range(G):
            yg = jnp.concatenate(ys[g * HPG:(g + 1) * HPG], axis=1) * _silu(zc[:, g * gw:(g + 1) * gw])
            outs.append(_rms(yg, gn[:, g * gw:(g + 1) * gw]))
        y_ref[0, pl.ds(r0, C), :] = jnp.concatenate(outs, axis=1)
        return carry

    lax.fori_loop(0, T // C, chunk, 0)
    h1_ref[0] = h_ref[...]


def _ssd(z, xbc_raw, dt_raw, conv_buf, conv_w, conv_b, a_log, dt_bias, d_skip, gnorm, h0, layer, C, t_real):
    B, T, _ = z.shape
    H = SSM_HEADS
    HP = dt_raw.shape[-1]

    def row(a):
        a = a.astype(F32).reshape(1, -1)
        return jnp.pad(a, ((0, 0), (0, HP - a.shape[1]))) if a.shape[1] == H else a

    tok = lambda w: pl.BlockSpec((1, T, w), lambda b: (b, 0, 0))
    cst = lambda r, w: pl.BlockSpec((r, w), lambda b: (0, 0))
    st = pl.BlockSpec((1, H, SSM_HEAD_DIM, SSM_STATE), lambda b: (b, 0, 0, 0))
    bufs = pl.BlockSpec((1, SSM_CONV - 1, SSM_XBC), lambda b: (b, 0, 0))
    return pl.pallas_call(
        functools.partial(_ssd_body, T=T, C=C, t_real=t_real),
        grid=(B,),
        in_specs=[tok(GROUP_W), tok(SSM_XBC), tok(HP),
                  pl.BlockSpec((1, 1, SSM_CONV - 1, SSM_XBC), lambda b: (layer, b, 0, 0)),
                  cst(SSM_CONV, SSM_XBC), cst(1, SSM_XBC),
                  cst(1, HP), cst(1, HP), cst(1, HP), cst(1, GROUP_W),
                  pl.BlockSpec((1, 1, H, SSM_HEAD_DIM, SSM_STATE), lambda b: (layer, b, 0, 0, 0))],
        out_specs=[tok(GROUP_W), st, bufs],
        out_shape=[jax.ShapeDtypeStruct((B, T, GROUP_W), F32),
                   jax.ShapeDtypeStruct((B, H, SSM_HEAD_DIM, SSM_STATE), F32),
                   jax.ShapeDtypeStruct((B, SSM_CONV - 1, SSM_XBC), F32)],
        scratch_shapes=[pltpu.VMEM((T + 2 * SUBLANE, SSM_XBC), F32),
                        pltpu.VMEM((H, SSM_HEAD_DIM, SSM_STATE), F32)],
        compiler_params=_cparams("parallel"),
        name="ssd",
    )(z, xbc_raw, dt_raw, conv_buf, conv_w.astype(F32), row(conv_b), row(-jnp.exp(a_log.astype(F32))),
      row(dt_bias), row(d_skip), row(gnorm), h0)


INT_MIN = -2 ** 31
NEG_INF_KEY = -2139095041


def _lane_count(m):
    x = jnp.where(m, 1.0, 0.0)
    tiles = [x[:, i:i + LANE] for i in range(0, x.shape[1], LANE)]
    while len(tiles) > 1:
        nxt = [tiles[i] + tiles[i + 1] for i in range(0, len(tiles) - 1, 2)]
        if len(tiles) % 2:
            nxt.append(tiles[-1])
        tiles = nxt
    return jnp.sum(tiles[0], axis=1, keepdims=True)


def _topk_mask(score, k, idx_bits, bits_per_iter=1):
    R, L = score.shape
    fold = 2 if R * 2 == SUBLANE else 1
    W = -(-L // (fold * LANE)) * LANE
    bits = lax.bitcast_convert_type(score, I32)
    key = bits ^ ((bits >> 31) & jnp.int32(0x7FFFFFFF))
    if fold * W > L:
        key = jnp.concatenate([key, jnp.full((R, fold * W - L), INT_MIN, I32)], axis=1)
    if fold > 1:
        key = jnp.concatenate([key[:, j * W:(j + 1) * W] for j in range(fold)], axis=0)
    RR = R * fold
    kf = jnp.float32(k)

    def count(m):
        c = _lane_count(m)
        return c + pltpu.roll(c, R, 0) if fold > 1 else c

    thr = jnp.where(count(key >= 0) >= kf, jnp.int32(0), jnp.int32(INT_MIN))
    lead = 31 % bits_per_iter
    for j in range(lead):
        cand = thr | jnp.int32(1 << (30 - j))
        thr = jnp.where(count(key >= cand) >= kf, cand, thr)
    n_iter = (31 - lead) // bits_per_iter

    def vbody(i, t):
        shift = (n_iter - 1 - i) * bits_per_iter
        step = jnp.zeros((RR, 1), I32)
        for j in range(1, 2 ** bits_per_iter):
            cand = t | (jnp.int32(j) << shift)
            step = step + jnp.where(count(key >= cand) >= kf, 1, 0).astype(I32)
        return t | (step << shift)

    thr = lax.fori_loop(0, n_iter, vbody, thr)
    gt = key > thr
    tie = key == thr
    need = kf - count(gt)
    idx = lax.broadcasted_iota(I32, (RR, W), 1)
    if fold > 1:
        idx = idx + (lax.broadcasted_iota(I32, (RR, W), 0) // R) * W

    def search_cut():
        def ibody(i, c):
            cand = c | (jnp.int32(1) << (idx_bits - 1 - i))
            return jnp.where(count(tie & (idx < cand)) < need, cand, c)
        return lax.fori_loop(0, idx_bits, ibody, jnp.zeros((RR, 1), I32))

    excess = (count(tie) > need) & (thr != NEG_INF_KEY)
    any_excess = jnp.max(jnp.where(excess, 1.0, 0.0)) > 0.5
    cut = lax.cond(any_excess, search_cut, lambda: jnp.full((RR, 1), 2 ** idx_bits - 1, I32))
    sel = gt | (tie & (idx <= cut))
    if fold == 1:
        return sel if W == L else sel[:, :L]
    self_ = jnp.where(sel, 1.0, 0.0)
    self_ = jnp.concatenate([self_[j * R:(j + 1) * R] for j in range(fold)], axis=1)[:, :L]
    return self_ > 0.5


def _dsa_prompt_body(q_ref, k_ref, v_ref, iq_ref, ik_ref, iw_ref, o_ref, *, tq, L, q0, topk, idx_bits):
    qi = q0 + pl.program_id(1)
    nt = (((1,), (1,)), ((), ()))
    ikb = ik_ref[0]
    iw = iw_ref[0]
    acc = jnp.zeros((tq, L), F32)
    for h in range(IDX_HEADS):
        s = lax.dot_general(iq_ref[0, h], ikb, nt, preferred_element_type=F32)
        acc = acc + iw[:, h:h + 1] * jnp.maximum(s, 0.0)
    qpos = qi * tq + lax.broadcasted_iota(I32, (tq, 1), 0)
    kpos = lax.broadcasted_iota(I32, (1, L), 1)
    causal = kpos <= qpos
    score = jnp.where(causal, acc * IDX_SCALE, NEG_INF)
    sel = _topk_mask(score, topk, idx_bits) & causal
    kb = k_ref[0]
    vb = v_ref[0]
    for h in range(DSA_HEADS):
        s = lax.dot_general(q_ref[0, h], kb, nt, preferred_element_type=F32) * (DSA_DH ** -0.5)
        s = jnp.where(sel, s, NEG_INF)
        p = jnp.exp(s - jnp.max(s, axis=1, keepdims=True))
        l = jnp.sum(p, axis=1, keepdims=True)
        o = jnp.dot(p.astype(BF16), vb, preferred_element_type=F32)
        o_ref[0, :, h * DSA_DH:(h + 1) * DSA_DH] = o / l


def _dsa_prompt(q, k, v, iq, ik, iw, tq=128, n_span=4):
    B, H, T, D = q.shape
    topk = min(DSA_TOPK, T // 4)
    nq = T // tq
    if nq % n_span:
        n_span = 1
    per = nq // n_span
    outs = []
    for c in range(n_span):
        L = (c + 1) * per * tq
        q0 = c * per
        outs.append(pl.pallas_call(
            functools.partial(_dsa_prompt_body, tq=tq, L=L, q0=q0, topk=topk,
                              idx_bits=max(1, (L - 1).bit_length())),
            grid=(B, per),
            in_specs=[
                pl.BlockSpec((1, H, tq, D), lambda b, i, q0=q0: (b, 0, q0 + i, 0)),
                pl.BlockSpec((1, L, D), lambda b, i: (b, 0, 0)),
                pl.BlockSpec((1, L, D), lambda b, i: (b, 0, 0)),
                pl.BlockSpec((1, IDX_HEADS, tq, IDX_DIM), lambda b, i, q0=q0: (b, 0, q0 + i, 0)),
                pl.BlockSpec((1, L, IDX_DIM), lambda b, i: (b, 0, 0)),
                pl.BlockSpec((1, tq, IDX_HEADS), lambda b, i, q0=q0: (b, q0 + i, 0)),
            ],
            out_specs=pl.BlockSpec((1, tq, H * D), lambda b, i: (b, i, 0)),
            out_shape=jax.ShapeDtypeStruct((B, per * tq, H * D), F32),
            compiler_params=_cparams("parallel", "parallel"),
            name="dsa_prompt",
        )(q, k, v, iq, ik, iw))
    return outs[0] if n_span == 1 else jnp.concatenate(outs, axis=1)


def _mqa_causal_body(q_ref, k_ref, v_ref, o_ref, *, tq, tk, H, scale):
    qi = pl.program_id(1)
    Dk = q_ref.shape[-1]
    Dv = v_ref.shape[-1]
    R = H * tq
    q = q_ref[0].reshape(R, Dk)
    qpos = qi * tq + lax.broadcasted_iota(I32, (R, 1), 0) % tq
    nkv = ((qi + 1) * tq + tk - 1) // tk

    def body(c, carry):
        m, l, acc = carry
        c0 = pl.multiple_of(c * tk, tk)
        kc = k_ref[0, pl.ds(c0, tk), :]
        vc = v_ref[0, pl.ds(c0, tk), :]
        s = lax.dot_general(q, kc, (((1,), (1,)), ((), ())), preferred_element_type=F32) * scale
        kpos = c0 + lax.broadcasted_iota(I32, (1, tk), 1)
        s = jnp.where(kpos <= qpos, s, NEG_INF)
        m_new = jnp.maximum(m, jnp.max(s, axis=1, keepdims=True))
        alpha = jnp.exp(m - m_new)
        p = jnp.exp(s - m_new)
        l = alpha * l + jnp.sum(p, axis=1, keepdims=True)
        acc = alpha * acc + jnp.dot(p.astype(BF16), vc, preferred_element_type=F32)
        return m_new, l, acc

    m0 = jnp.full((R, 1), NEG_INF, F32)
    m, l, acc = lax.fori_loop(0, nkv, body, (m0, jnp.zeros((R, 1), F32), jnp.zeros((R, Dv), F32)))
    o = acc / l
    for h in range(H):
        o_ref[0, :, h * Dv:(h + 1) * Dv] = o[h * tq:(h + 1) * tq]


def _mqa_causal(q, k, v, scale, tq=128, tk=512):
    B, H, T, Dk = q.shape
    Dv = v.shape[-1]
    tk = min(tk, T)
    return pl.pallas_call(
        functools.partial(_mqa_causal_body, tq=tq, tk=tk, H=H, scale=scale),
        grid=(B, T // tq),
        in_specs=[
            pl.BlockSpec((1, H, tq, Dk), lambda b, i: (b, 0, i, 0)),
            pl.BlockSpec((1, T, Dk), lambda b, i: (b, 0, 0)),
            pl.BlockSpec((1, T, Dv), lambda b, i: (b, 0, 0)),
        ],
        out_specs=pl.BlockSpec((1, tq, H * Dv), lambda b, i: (b, i, 0)),
        out_shape=jax.ShapeDtypeStruct((B, T, H * Dv), F32),
        compiler_params=_cparams("parallel", "parallel"),
        name="mqa_causal",
    )(q, k, v)


def _mem_attn_body(q_ref, k_ref, v_ref, o_ref):
    D = MEM_DH
    for h in range(MEM_HEADS):
        q = q_ref[0, :, h * D:(h + 1) * D].astype(BF16)
        kh = k_ref[0, 0, :, h * D:(h + 1) * D].astype(BF16)
        vh = v_ref[0, 0, :, h * D:(h + 1) * D].astype(BF16)
        s = lax.dot_general(q, kh, (((1,), (1,)), ((), ())), preferred_element_type=F32) * (D ** -0.5)
        p = jnp.exp(s - jnp.max(s, axis=1, keepdims=True))
        l = jnp.sum(p, axis=1, keepdims=True)
        o_ref[0, :, h * D:(h + 1) * D] = jnp.dot(p.astype(BF16), vh, preferred_element_type=F32) / l


def _mem_attn(q, mk, mv, layer, tq_cap=512):
    B, T, W = q.shape
    tq = _pick_tile(T, tq_cap, SUBLANE)
    N = mk.shape[2]
    return pl.pallas_call(
        _mem_attn_body,
        grid=(B, T // tq),
        in_specs=[pl.BlockSpec((1, tq, W), lambda b, i: (b, i, 0)),
                  pl.BlockSpec((1, 1, N, W), lambda b, i: (layer, b, 0, 0)),
                  pl.BlockSpec((1, 1, N, W), lambda b, i: (layer, b, 0, 0))],
        out_specs=pl.BlockSpec((1, tq, W), lambda b, i: (b, i, 0)),
        out_shape=jax.ShapeDtypeStruct((B, T, W), F32),
        compiler_params=_cparams("parallel", "parallel"),
        name="mem_attn",
    )(q, mk, mv)


N_PAGED = 5


def _paged_body(pt_ref, iq_ref, iw_ref, dq_ref, mq_ref, ikn_ref, dkn_ref, dvn_ref, cn_ref, krn_ref,
                cik, ck, cv, cc, ckr, yb_ref, ol_ref, ik_buf, k_buf, v_buf, c_buf, kr_buf, sem,
                *, layer, n_pages, T, topk, idx_bits, chunk):
    b = pl.program_id(0)
    nb = pl.num_programs(0)
    slot = b % 2
    past = n_pages * PAGE_SIZE
    LT = past + LANE
    caches = (cik, ck, cv, cc, ckr)
    bufs = (ik_buf, k_buf, v_buf, c_buf, kr_buf)
    news = (ikn_ref, dkn_ref, dvn_ref, cn_ref, krn_ref)
    key_on_lanes = (True, False, False, False, True)

    def page_copy(a, seq, sl, p):
        keys = pl.ds(p * PAGE_SIZE, PAGE_SIZE)
        dst = bufs[a].at[sl, :, keys] if key_on_lanes[a] else bufs[a].at[sl, keys]
        return pltpu.make_async_copy(caches[a].at[layer, pt_ref[seq, p]], dst, sem.at[sl, a])

    def for_pages(seq, sl, start):
        def body(p, carry):
            for a in range(N_PAGED):
                cp = page_copy(a, seq, sl, p)
                cp.start() if start else cp.wait()
            return carry
        lax.fori_loop(0, n_pages, body, 0)

    @pl.when(b == 0)
    def _():
        for a in range(N_PAGED):
            if not key_on_lanes[a]:
                bufs[a][:, past:LT, :] = jnp.zeros((2, LANE, bufs[a].shape[-1]), F32)
        for_pages(0, 0, True)

    @pl.when(b + 1 < nb)
    def _():
        for_pages(b + 1, 1 - slot, True)

    for_pages(b, slot, False)
    for a in range(N_PAGED):
        if key_on_lanes[a]:
            bufs[a][slot, :, past:LT] = news[a][0]
        else:
            bufs[a][slot, past:past + SUBLANE, :] = news[a][0]

    nt = (((1,), (1,)), ((), ()))
    nn = (((1,), (0,)), ((), ()))
    starts = list(range(0, LT, chunk))
    tpos = lax.broadcasted_iota(I32, (T, 1), 0)
    kpos = lax.broadcasted_iota(I32, (1, LT), 1)
    valid = (kpos < past) | ((kpos - past) <= tpos)

    def keys_of(a, c0):
        if key_on_lanes[a]:
            return bufs[a][slot, :, c0:c0 + chunk].astype(BF16), nn
        return bufs[a][slot, c0:c0 + chunk, :].astype(BF16), nt

    iq = iq_ref[0]
    iw = iw_ref[0]
    parts = []
    for c0 in starts:
        ikc, dn = keys_of(0, c0)
        s = lax.dot_general(iq, ikc, dn, preferred_element_type=F32)
        s = jnp.maximum(s, 0.0) * iw
        parts.append(jnp.sum(s.reshape(T, IDX_HEADS, chunk), axis=1))
    score = jnp.where(valid, jnp.concatenate(parts, axis=1) * IDX_SCALE, NEG_INF)
    sel = _topk_mask(score, topk, idx_bits, bits_per_iter=3) & valid

    def attend(q_parts, key_ids, val_buf, mask, scale, out_ref):
        bias = jnp.where(mask, 0.0, NEG_INF)
        bias = jnp.concatenate([bias] * (q_parts[0].shape[0] // T), axis=0)
        sp = []
        for c0 in starts:
            s = None
            for qp, a in zip(q_parts, key_ids):
                kc, dn = keys_of(a, c0)
                t = lax.dot_general(qp, kc, dn, preferred_element_type=F32)
                s = t if s is None else s + t
            sp.append(s)
        s = jnp.concatenate(sp, axis=1) * scale + bias
        p = jnp.exp(s - jnp.max(s, axis=1, keepdims=True))
        l = jnp.sum(p, axis=1, keepdims=True)
        o = None
        for c0 in starts:
            t = jnp.dot(p[:, c0:c0 + chunk].astype(BF16), val_buf[slot, c0:c0 + chunk, :].astype(BF16),
                        preferred_element_type=F32)
            o = t if o is None else o + t
        out_ref[0] = o / l

    attend([dq_ref[0]], [1], v_buf, sel, DSA_DH ** -0.5, yb_ref)
    mq = mq_ref[0]
    attend([mq[:, :MLA_KVLORA], mq[:, MLA_KVLORA:MLA_KVLORA + MLA_ROPE]], [3, 4], c_buf,
           valid, MLA_SCALE, ol_ref)


def _paged_attn(page_table, iq, iw, dq, mq, ikn, dkn, dvn, cn, krn, cik, ck, cv, cc, ckr, layer, T):
    B, n_pages = page_table.shape
    past = n_pages * PAGE_SIZE
    LT = past + LANE
    topk = min(DSA_TOPK, (past + T) // 4)
    nchunk = 1
    for cand in (5, 4, 3, 2):
        if (LT // LANE) % cand == 0 and LT // cand >= 512:
            nchunk = cand
            break
    per_seq = lambda a: pl.BlockSpec((1,) + a.shape[1:], lambda b, pt: (b, 0, 0))
    anyspec = pl.BlockSpec(memory_space=pl.ANY)
    vm_in = [iq, iw, dq, mq, ikn, dkn, dvn, cn, krn]
    grid_spec = pltpu.PrefetchScalarGridSpec(
        num_scalar_prefetch=1,
        grid=(B,),
        in_specs=[per_seq(a) for a in vm_in] + [anyspec] * N_PAGED,
        out_specs=[pl.BlockSpec((1, dq.shape[1], DSA_DH), lambda b, pt: (b, 0, 0)),
                   pl.BlockSpec((1, mq.shape[1], MLA_KVLORA), lambda b, pt: (b, 0, 0))],
        scratch_shapes=[pltpu.VMEM((2, cik.shape[2], LT), F32)]
        + [pltpu.VMEM((2, LT, c.shape[-1]), F32) for c in (ck, cv, cc)]
        + [pltpu.VMEM((2, ckr.shape[2], LT), F32), pltpu.SemaphoreType.DMA((2, N_PAGED))],
    )
    return pl.pallas_call(
        functools.partial(_paged_body, layer=layer, n_pages=n_pages, T=T, topk=topk,
                          idx_bits=max(1, (LT - 1).bit_length()), chunk=LT // nchunk),
        grid_spec=grid_spec,
        out_shape=[jax.ShapeDtypeStruct((B, dq.shape[1], DSA_DH), F32),
                   jax.ShapeDtypeStruct((B, mq.shape[1], MLA_KVLORA), F32)],
        compiler_params=_cparams("arbitrary"),
        name="paged_attn",
    )(page_table, *vm_in, cik, ck, cv, cc, ckr)


def _in_layout():
    names = ("hq", "hf", "hi", "hg", "dq", "dk", "dv", "iq", "ik", "iw", "sz", "sxbc", "sdt", "mcq", "mckv", "mkr")
    src, o = {}, 0
    for n, w in zip(names, IN_WIDTHS):
        src[n] = (o, w)
        o += w
    groups = (("hq",), ("hf",), ("hi",), ("hg",), ("dq",), ("dk",), ("dv",), ("iq",), ("ik", "iw"),
              ("sz",), ("sxbc",), ("sdt",), ("mcq",), ("mckv",), ("mkr",))
    pieces, dst, o = [], {}, 0
    for g in groups:
        w = 0
        for n in g:
            dst[n] = (o + w, src[n][1])
            pieces.append(("copy",) + src[n])
            w += src[n][1]
        pad = (-w) % LANE
        if pad:
            pieces.append(("zero", 0, pad))
        o += w + pad
    return pieces, dst, o


_IN_PIECES, _IN_DST, D_IN_PAD = _in_layout()


def _pad_w_in(w):
    cols = []
    for kind, o, n in _IN_PIECES:
        cols.append(w[:, o:o + n] if kind == "copy" else jnp.zeros((w.shape[0], n), w.dtype))
    return jnp.concatenate(cols, axis=1).astype(BF16)


def _part(u, name):
    o, w = _IN_DST[name]
    return u[..., o:o + w]


def _rope_cos_sin(pos, rot_dim):
    inv = ROPE_THETA ** (-jnp.arange(0, rot_dim, 2, dtype=F32) / rot_dim)
    ang = pos.astype(F32)[:, None] * inv[None, :]
    return jnp.cos(ang), jnp.sin(ang)


def _rotate(x, cos, sin):
    r2 = cos.shape[-1]
    x1, x2 = x[..., :r2], x[..., r2:2 * r2]
    c, s = cos[:, None, :], sin[:, None, :]
    return jnp.concatenate([x1 * c - x2 * s, x1 * s + x2 * c], axis=-1)


def _partial_rotary(x, cos, sin):
    r = 2 * cos.shape[-1]
    return jnp.concatenate([_rotate(x[..., :r], cos, sin), x[..., r:]], axis=-1)


def _pad_rows(a, rows):
    return jnp.pad(a, ((0, 0), (0, rows - a.shape[1])) + ((0, 0),) * (a.ndim - 2))


def _heads_first(a):
    return jnp.swapaxes(a, 1, 2)


def kernel(x_prompt, x_sample, cache_dsa_k, cache_dsa_v, cache_dsa_ik, cache_mla_c, cache_mla_kr, cache_mem_k, cache_mem_v, state_hgrn, state_ssm, state_ssm_conv, state_ffn_conv, page_table, mem_prompt, w_in, w_out, mix_norm, hgrn_lb, hgrn_norm, mla_q_norm, mla_kv_norm, mla_w_uq, mla_w_uk, mla_w_uv, ssm_conv_w, ssm_conv_b, ssm_a_log, ssm_dt_bias, ssm_d, ssm_norm, mem_norm, cross_norm, cross_wq, cross_wk, cross_wv, cross_wo, ffn_norm, ffn_w_up, ffn_conv_w, ffn_conv_b, ffn_w_down, final_norm):
    Bp, Tp, D = x_prompt.shape
    Bs, Ts, _ = x_sample.shape
    Mp, Ms = Bp * Tp, Bs * Ts
    depth = w_in.shape[0]
    past = page_table.shape[1] * PAGE_SIZE
    n_mem = mem_prompt.shape[1]
    TS_PAD = 16

    lb = jnp.cumsum(jax.nn.softmax(hgrn_lb.astype(F32), axis=0), axis=0)
    lb = lb - lb[:1]

    pos_p = jnp.arange(Tp, dtype=jnp.int32)
    pos_s = past + jnp.arange(Ts, dtype=jnp.int32)
    rope = {}
    for name, pos in (("p", pos_p), ("s", pos_s)):
        rope[name] = (_rope_cos_sin(pos, DSA_DH // 4), _rope_cos_sin(pos, IDX_DIM // 4), _rope_cos_sin(pos, MLA_ROPE))

    x = jnp.concatenate([x_prompt.reshape(Mp, D), x_sample.reshape(Ms, D)], axis=0)
    p_rows, s_rows, p_mk, p_mv = [], [], [], []

    def split(a):
        return a[:Mp].reshape(Bp, Tp, -1), a[Mp:].reshape(Bs, Ts, -1)

    for l in range(depth):
        w_in_l = _pad_w_in(w_in[l])
        u = _mm(x, w_in_l, gain=mix_norm[l], name="mm_in")

        q_d = _mm(_part(u, "mcq"), mla_w_uq[l].astype(BF16), gain=mla_q_norm[l], name="mm_uq")
        uk = jnp.pad(jnp.transpose(mla_w_uk[l], (1, 2, 0)), ((0, 0), (0, MLA_ROPE), (0, 0)))
        uk_bd = jax.scipy.linalg.block_diag(*[uk[h] for h in range(MLA_HEADS)]).astype(BF16)
        uv_bd = jax.scipy.linalg.block_diag(*[mla_w_uv[l][:, h, :] for h in range(MLA_HEADS)]).astype(BF16)
        q_lat = _mm(q_d, uk_bd, name="mm_uk")
        c_kv = _rmsnorm(_part(u, "mckv"), mla_kv_norm[l])

        ys, olats, rows_by_group = [], [], []
        for gi, (qd_g, ql_g, ckv_g) in enumerate(zip(split(q_d), split(q_lat), split(c_kv))):
            prompt = gi == 0
            B, T = (Bp, Tp) if prompt else (Bs, Ts)
            r0, r1 = (0, Mp) if prompt else (Mp, Mp + Ms)

            def gp(name, width=None, r0=r0, r1=r1, B=B, T=T):
                o, w = _IN_DST[name]
                w = width or w
                return u[r0:r1, o:o + w].reshape(B, T, w)

            (cos_d, sin_d), (cos_i, sin_i), (cos_m, sin_m) = rope["p" if prompt else "s"]
            hq, hf, hi, hg = (gp(n) for n in ("hq", "hf", "hi", "hg"))
            sz, sxbc, sdt = gp("sz"), gp("sxbc"), gp("sdt", LANE)
            dv = gp("dv")
            q_b = _partial_rotary(gp("dq").reshape(B, T, DSA_HEADS, DSA_DH), cos_d, sin_d)
            k_b = _partial_rotary(gp("dk").reshape(B, T, 1, DSA_DH), cos_d, sin_d)[:, :, 0]
            iq_b = _partial_rotary(gp("iq").reshape(B, T, IDX_HEADS, IDX_DIM), cos_i, sin_i)
            ik_b = _partial_rotary(gp("ik").reshape(B, T, 1, IDX_DIM), cos_i, sin_i)[:, :, 0]
            iw = gp("iw")
            q_rope = _rotate(qd_g.reshape(B, T, MLA_HEADS, MLA_NOPE + MLA_ROPE)[..., MLA_NOPE:], cos_m, sin_m)
            k_rope = _rotate(gp("mkr").reshape(B, T, 1, MLA_ROPE), cos_m, sin_m)[:, :, 0]
            q_aug = jnp.concatenate([ql_g.reshape(B, T, MLA_HEADS, MLA_KVLORA), q_rope,
                                     jnp.zeros((B, T, MLA_HEADS, LANE - MLA_ROPE), F32)], axis=-1)
            q_aug = _heads_first(q_aug).astype(BF16)

            if prompt:
                y_a, hg1 = _hgrn(hq, hf, hi, hg, lb[l], hgrn_norm[l],
                                 jnp.zeros((1, B, HG_HEADS, HG_DK, HG_DK), F32), 0, HG_CHUNK, T, 1)
                y_c, ssm1, sconv1 = _ssd(sz, sxbc, sdt, jnp.zeros((1, B, SSM_CONV - 1, SSM_XBC), F32),
                                         ssm_conv_w[l], ssm_conv_b[l], ssm_a_log[l], ssm_dt_bias[l], ssm_d[l],
                                         ssm_norm[l], jnp.zeros((1, B, SSM_HEADS, SSM_HEAD_DIM, SSM_STATE), F32),
                                         0, min(SSM_CHUNK, T), T)
                y_b = _dsa_prompt(_heads_first(q_b).astype(BF16), k_b.astype(BF16), dv.astype(BF16),
                                  _heads_first(iq_b).astype(BF16), ik_b.astype(BF16), iw)
                k_aug = jnp.concatenate([ckv_g, k_rope, jnp.zeros((B, T, LANE - MLA_ROPE), F32)], axis=-1)
                o_lat = _mqa_causal(q_aug, k_aug.astype(BF16), ckv_g.astype(BF16), MLA_SCALE)
            else:
                pt = lambda a: _pad_rows(a, TS_PAD)
                nb = 4 if B % 4 == 0 else 1
                y_a, hg1 = _hgrn(pt(hq), pt(hf), pt(hi), pt(hg), lb[l], hgrn_norm[l], state_hgrn, l,
                                 HG_CHUNK, T, nb)
                y_c, ssm1, sconv1 = _ssd(pt(sz), pt(sxbc), pt(sdt), state_ssm_conv, ssm_conv_w[l],
                                         ssm_conv_b[l], ssm_a_log[l], ssm_dt_bias[l], ssm_d[l], ssm_norm[l],
                                         state_ssm, l, TS_PAD, T)
                y_a, y_c = y_a[:, :T], y_c[:, :T]
                pr = lambda a: _pad_rows(a, SUBLANE)
                pc = lambda a: jnp.pad(jnp.swapaxes(a, 1, 2), ((0, 0), (0, 0), (0, LANE - T)))
                yb, ol = _paged_attn(
                    page_table,
                    iq_b.reshape(B, T * IDX_HEADS, IDX_DIM).astype(BF16), iw.reshape(B, T * IDX_HEADS, 1),
                    _heads_first(q_b).reshape(B, DSA_HEADS * T, DSA_DH).astype(BF16),
                    q_aug.reshape(B, MLA_HEADS * T, 2 * LANE),
                    pc(ik_b), pr(k_b), pr(dv), pr(ckv_g), pc(k_rope),
                    jnp.swapaxes(cache_dsa_ik, 2, 3), cache_dsa_k, cache_dsa_v, cache_mla_c,
                    jnp.swapaxes(cache_mla_kr, 2, 3), l, T)
                y_b = _heads_first(yb.reshape(B, DSA_HEADS, T, DSA_DH)).reshape(B, T, GROUP_W)
                o_lat = _heads_first(ol.reshape(B, MLA_HEADS, T, MLA_KVLORA)).reshape(B, T, MLA_HEADS * MLA_KVLORA)
            ys.append((y_a, y_b, y_c))
            olats.append(o_lat.reshape(B * T, -1))
            rows_by_group.append([k_b, dv, ik_b, ckv_g, k_rope, hg1, ssm1, sconv1])

        y_d = _mm(jnp.concatenate(olats, axis=0), uv_bd, name="mm_uv")
        y_abc = [jnp.concatenate([g[i].reshape(-1, GROUP_W) for g in ys], axis=0) for i in range(3)]
        x = _mm(jnp.concatenate(y_abc + [y_d], axis=1), w_out[l].astype(BF16), res=x, name="mm_out")

        qc_p, qc_s = split(_mm(x, cross_wq[l].astype(BF16), gain=cross_norm[l], name="mm_cq"))
        mem = mem_prompt.reshape(Bp * n_mem, D)
        mk = _mm(mem, cross_wk[l].astype(BF16), gain=mem_norm[l], name="mm_ck").reshape(Bp, n_mem, MEM_HEADS * MEM_DH)
        mv = _mm(mem, cross_wv[l].astype(BF16), gain=mem_norm[l], name="mm_cv").reshape(Bp, n_mem, MEM_HEADS * MEM_DH)
        att_p = _mem_attn(qc_p, mk[None], mv[None], 0)
        att_s = _mem_attn(qc_s, cache_mem_k.reshape(depth, Bs, -1, MEM_HEADS * MEM_DH),
                          cache_mem_v.reshape(depth, Bs, -1, MEM_HEADS * MEM_DH), l)
        att = jnp.concatenate([att_p.reshape(Mp, -1), att_s.reshape(Ms, -1)], axis=0)
        x = _mm(att, cross_wo[l].astype(BF16), res=x, name="mm_co")
        p_mk.append(mk.reshape(Bp, n_mem, MEM_HEADS, MEM_DH))
        p_mv.append(mv.reshape(Bp, n_mem, MEM_HEADS, MEM_DH))

        w_up = ffn_w_up[l].astype(BF16)
        h_p, fconv_p = _ffn_up(x, ffn_norm[l], w_up, ffn_conv_w[l], ffn_conv_b[l],
                               jnp.zeros((Bp, FFN_CONV - 1, D_FF), F32), Tp, 0, Mp)
        prev_s = jnp.broadcast_to(jnp.swapaxes(state_ffn_conv[l], 0, 1)[:, :, None, :],
                                  (FFN_CONV - 1, Bs, Ts, D_FF)).reshape(FFN_CONV - 1, Ms, D_FF)
        h_s, a_s = _ffn_up(x, ffn_norm[l], w_up, ffn_conv_w[l], ffn_conv_b[l], prev_s, Ts, Mp, Ms)
        fconv_s = a_s.reshape(Bs, Ts, D_FF)[:, Ts - (FFN_CONV - 1):]
        x = _mm(jnp.concatenate([h_p, h_s], axis=0), ffn_w_down[l].astype(BF16), res=x, tn_cap=512,
                name="mm_down")

        p_rows.append(rows_by_group[0] + [fconv_p])
        s_rows.append(rows_by_group[1] + [fconv_s])

    y = _rmsnorm(x, final_norm)
    y_prompt, y_sample = y[:Mp].reshape(Bp, Tp, D), y[Mp:].reshape(Bs, Ts, D)
    stack = lambda rows, i: jnp.stack([r[i] for r in rows], axis=0)
    return ((y_prompt, y_sample)
            + tuple(stack(p_rows, i) for i in range(9))
            + (jnp.stack(p_mk, axis=0), jnp.stack(p_mv, axis=0))
            + tuple(stack(s_rows, i) for i in range(9)))
```
